```python
import math
import jax, jax.numpy as jnp
from jax import lax
import numpy as np

D_MODEL = 4096
BATCH = 4
SEQ = 2048
DEPTH = 4
DEC_BATCH = 32
DEC_SEQ = 1
PAST_LEN = 8192
PAGE_SIZE = 128

MIX_WIDTH = D_MODEL
GROUP_W = MIX_WIDTH // 4
HEAD_DIM = 64
N_HEADS = GROUP_W // HEAD_DIM
N_KV = 4
Q_PER_KV = N_HEADS // N_KV
WINDOW = 128
CONV_A_W = 3
CONV_C_W = 31
POOL_WINDOWS = (2, 4, 8, 16)
N_POOL_GROUPS = len(POOL_WINDOWS)
POOL_GROUP_W = GROUP_W // N_POOL_GROUPS
POOL_STATE = max(POOL_WINDOWS) - 1
NUM_BUCKETS = 32
MAX_DISTANCE = 128
D_FF = ((8 * D_MODEL // 3 + 255) // 256) * 256
EPS = 1e-6
SPLITS = (GROUP_W, GROUP_W, GROUP_W,
          GROUP_W,
          2 * GROUP_W,
          N_HEADS * HEAD_DIM, N_KV * HEAD_DIM, N_KV * HEAD_DIM)
IN_COLS = sum(SPLITS)
SPLIT_IDX = tuple(int(c) for c in np.cumsum(SPLITS)[:-1])

kernel_name = "hybrid_parallel_conv_pool_swa_decoder_step"


def rms_norm(x, g):
    xf = x.astype(jnp.float32)
    y = xf * lax.rsqrt(jnp.mean(xf * xf, axis=-1, keepdims=True) + EPS)
    return (y * g.astype(jnp.float32)).astype(x.dtype)


def layer_norm(x, g, b):
    xf = x.astype(jnp.float32)
    mu = jnp.mean(xf, axis=-1, keepdims=True)
    var = jnp.mean(jnp.square(xf - mu), axis=-1, keepdims=True)
    y = (xf - mu) * lax.rsqrt(var + EPS)
    return (y * g.astype(jnp.float32) + b.astype(jnp.float32)).astype(x.dtype)


def causal_dwconv(u, prev, w):
    k_w = w.shape[0]
    full = jnp.concatenate([prev.astype(u.dtype), u], axis=1)
    y = lax.conv_general_dilated(full, w[:, None, :].astype(u.dtype), window_strides=(1,),
                                 padding="VALID", dimension_numbers=("NWC", "WIO", "NWC"),
                                 feature_group_count=u.shape[-1])
    return y, full[:, full.shape[1] - (k_w - 1):]


def pool_mix(u, prev, start_pos, w_pool, scale):
    bsz, t_len, c = u.shape
    full = jnp.concatenate([prev.astype(u.dtype), u], axis=1)
    cs = lax.cumsum(full.astype(jnp.float32), axis=1)
    cs = jnp.pad(cs, ((0, 0), (1, 0), (0, 0)))
    pos = start_pos + jnp.arange(t_len, dtype=jnp.int32)
    p = POOL_STATE
    means = []
    for g, win in enumerate(POOL_WINDOWS):
        sl = slice(g * POOL_GROUP_W, (g + 1) * POOL_GROUP_W)
        s = cs[:, p + 1:p + 1 + t_len, sl] - cs[:, p + 1 - win:p + 1 - win + t_len, sl]
        cnt = jnp.minimum(pos + 1, win).astype(jnp.float32)[None, :, None]
        means.append(s / cnt)
    mean = jnp.stack(means, axis=2)
    d = (mean - u.astype(jnp.float32).reshape(bsz, t_len, N_POOL_GROUPS, POOL_GROUP_W)).astype(u.dtype)
    y = jnp.einsum("btgc,gcd->btgd", d, w_pool).reshape(bsz, t_len, c) * scale
    return y, full[:, -p:]


def rel_bucket(rel):
    n = jnp.maximum(rel, 0)
    max_exact = NUM_BUCKETS // 2
    nf = jnp.maximum(n, 1).astype(jnp.float32)
    large = max_exact + (jnp.log(nf / max_exact) / math.log(MAX_DISTANCE / max_exact)
                         * (NUM_BUCKETS - max_exact)).astype(jnp.int32)
    large = jnp.minimum(large, NUM_BUCKETS - 1)
    return jnp.where(n < max_exact, n, large)


def sw_attend(q, k, v, q_pos, k_pos, rel_bias, sinks):
    bsz, nblk, nq, nh, hd = q.shape
    ns = k.shape[2]
    qg = q.reshape(bsz, nblk, nq, N_KV, Q_PER_KV, hd)
    s = jnp.einsum("bnqkgd,bnskd->bnkgqs", qg, k, preferred_element_type=jnp.float32) * (hd ** -0.5)
    rel = q_pos[:, :, None] - k_pos[:, None, :]
    valid = (rel >= 0) & (rel < WINDOW) & (k_pos[:, None, :] >= 0)
    bias = rel_bias.astype(jnp.float32)[rel_bucket(rel)]
    bias = bias.reshape(nblk, nq, ns, N_KV, Q_PER_KV).transpose(0, 3, 4, 1, 2)
    s = jnp.where(valid[:, None, None], s + bias, -jnp.inf)
    sink = sinks.astype(jnp.float32).reshape(N_KV, Q_PER_KV)[None, None, :, :, None, None]
    m = jnp.maximum(jnp.max(s, axis=-1, keepdims=True), sink)
    p = jnp.exp(s - m)
    p = (p / (jnp.sum(p, axis=-1, keepdims=True) + jnp.exp(sink - m))).astype(v.dtype)
    o = jnp.einsum("bnkgqs,bnskd->bnqkgd", p, v)
    return o.reshape(bsz, nblk * nq, nh * hd)


def attn_prompt(q, k, v, rel_bias, sinks):
    bsz, t_len = q.shape[:2]
    nb = t_len // WINDOW
    qb = q.reshape(bsz, nb, WINDOW, N_HEADS, HEAD_DIM)
    kb = k.reshape(bsz, nb, WINDOW, N_KV, HEAD_DIM)
    vb = v.reshape(bsz, nb, WINDOW, N_KV, HEAD_DIM)
    kprev = jnp.concatenate([jnp.zeros_like(kb[:, :1]), kb[:, :-1]], axis=1)
    vprev = jnp.concatenate([jnp.zeros_like(vb[:, :1]), vb[:, :-1]], axis=1)
    kc = jnp.concatenate([kprev, kb], axis=2)
    vc = jnp.concatenate([vprev, vb], axis=2)
    q_pos = jnp.arange(nb, dtype=jnp.int32)[:, None] * WINDOW + jnp.arange(WINDOW, dtype=jnp.int32)[None]
    k_pos = jnp.concatenate([q_pos - WINDOW, q_pos], axis=1)
    o = sw_attend(qb, kc, vc, q_pos, k_pos, rel_bias, sinks)
    wb = min(WINDOW, t_len)
    return o, k[:, t_len - wb:], v[:, t_len - wb:]


def attn_sample(q, k, v, k_buf, v_buf, rel_bias, sinks):
    wb = k_buf.shape[1]
    t_len = q.shape[1]
    kf = jnp.concatenate([k_buf.astype(k.dtype), k], axis=1)
    vf = jnp.concatenate([v_buf.astype(v.dtype), v], axis=1)
    q_pos = (wb + jnp.arange(t_len, dtype=jnp.int32))[None]
    k_pos = jnp.arange(wb + t_len, dtype=jnp.int32)[None]
    o = sw_attend(q[:, None], kf[:, None], vf[:, None], q_pos, k_pos, rel_bias, sinks)
    return o, kf[:, t_len:], vf[:, t_len:]


def trunk_layer(x, lw, conv_a_prev, pool_prev, conv_c_prev, start_pos, k_buf, v_buf, rel_bias):
    (norm_mix_g, w_in, conv_a_w, pool_w, pool_scale, conv_c_w, conv_c_b, ln_c_g, ln_c_b,
     q_norm_g, k_norm_g, sinks, w_out, norm_ffn_g, w_gate, w_up, w_down) = lw
    bsz, t_len, _ = x.shape
    h = rms_norm(x, norm_mix_g)
    z = h @ w_in
    ha, ba, ca, ub, gc, q, k, v = jnp.split(z, SPLIT_IDX, axis=-1)
    ya, conv_a_new = causal_dwconv(ca * ha, conv_a_prev, conv_a_w)
    ya = ba * ya
    yb, pool_new = pool_mix(ub, pool_prev, start_pos, pool_w, pool_scale)
    a_c, g_c = jnp.split(gc, 2, axis=-1)
    uc = a_c * jax.nn.sigmoid(g_c)
    yc, conv_c_new = causal_dwconv(uc, conv_c_prev, conv_c_w)
    yc = jax.nn.silu(layer_norm(yc + conv_c_b, ln_c_g, ln_c_b))
    q = rms_norm(q.reshape(bsz, t_len, N_HEADS, HEAD_DIM), q_norm_g)
    k = rms_norm(k.reshape(bsz, t_len, N_KV, HEAD_DIM), k_norm_g)
    v = v.reshape(bsz, t_len, N_KV, HEAD_DIM)
    if k_buf is None:
        yd, k_new, v_new = attn_prompt(q, k, v, rel_bias, sinks)
    else:
        yd, k_new, v_new = attn_sample(q, k, v, k_buf, v_buf, rel_bias, sinks)
    x = x + jnp.concatenate([ya, yb, yc, yd], axis=-1) @ w_out
    h2 = rms_norm(x, norm_ffn_g)
    x = x + (jax.nn.silu(h2 @ w_gate) * (h2 @ w_up)) @ w_down
    return x, (conv_a_new, pool_new, conv_c_new, k_new, v_new)


def setup_inputs(seed: int = 0) -> dict:
    key = jax.random.key(seed)
    ks = jax.random.split(key, 26)
    f32 = jnp.float32
    nrm = lambda k, shape, s: jax.random.normal(k, shape, f32) * s
    win_buf = min(WINDOW, PAST_LEN)
    return {
        "x_prompt": nrm(ks[0], (BATCH, SEQ, D_MODEL), 1.0),
        "x_sample": nrm(ks[1], (DEC_BATCH, DEC_SEQ, D_MODEL), 1.0),
        "state_conv_a": nrm(ks[2], (DEPTH, DEC_BATCH, CONV_A_W - 1, GROUP_W), 1.0),
        "state_pool": nrm(ks[3], (DEPTH, DEC_BATCH, POOL_STATE, GROUP_W), 1.0),
        "state_conv_c": nrm(ks[4], (DEPTH, DEC_BATCH, CONV_C_W - 1, GROUP_W), 0.5),
        "cache_k_win": nrm(ks[5], (DEPTH, DEC_BATCH, win_buf, N_KV, HEAD_DIM), 1.0),
        "cache_v_win": nrm(ks[6], (DEPTH, DEC_BATCH, win_buf, N_KV, HEAD_DIM), 1.0),
        "rel_bias": nrm(ks[7], (NUM_BUCKETS, N_HEADS), 0.5),
        "norm_mix_g": 1.0 + nrm(ks[8], (DEPTH, D_MODEL), 0.02),
        "w_in": nrm(ks[9], (DEPTH, D_MODEL, IN_COLS), D_MODEL ** -0.5),
        "conv_a_w": nrm(ks[10], (DEPTH, CONV_A_W, GROUP_W), CONV_A_W ** -0.5),
        "pool_w": nrm(ks[11], (DEPTH, N_POOL_GROUPS, POOL_GROUP_W, POOL_GROUP_W), POOL_GROUP_W ** -0.5),
        "pool_scale": 1.0 + nrm(ks[12], (DEPTH, GROUP_W), 0.02),
        "conv_c_w": nrm(ks[13], (DEPTH, CONV_C_W, GROUP_W), CONV_C_W ** -0.5),
        "conv_c_b": nrm(ks[14], (DEPTH, GROUP_W), 0.02),
        "ln_c_g": 1.0 + nrm(ks[15], (DEPTH, GROUP_W), 0.02),
        "ln_c_b": nrm(ks[16], (DEPTH, GROUP_W), 0.02),
        "q_norm_g": 1.0 + nrm(ks[17], (DEPTH, HEAD_DIM), 0.02),
        "k_norm_g": 1.0 + nrm(ks[18], (DEPTH, HEAD_DIM), 0.02),
        "attn_sinks": nrm(ks[19], (DEPTH, N_HEADS), 0.5),
        "w_out": nrm(ks[20], (DEPTH, MIX_WIDTH, D_MODEL), MIX_WIDTH ** -0.5),
        "norm_ffn_g": 1.0 + nrm(ks[21], (DEPTH, D_MODEL), 0.02),
        "w_gate": nrm(ks[22], (DEPTH, D_MODEL, D_FF), D_MODEL ** -0.5),
        "w_up": nrm(ks[23], (DEPTH, D_MODEL, D_FF), D_MODEL ** -0.5),
        "w_down": nrm(ks[24], (DEPTH, D_FF, D_MODEL), D_FF ** -0.5),
    }


def reference(x_prompt, x_sample, state_conv_a, state_pool, state_conv_c, cache_k_win, cache_v_win,
              rel_bias, norm_mix_g, w_in, conv_a_w, pool_w, pool_scale, conv_c_w, conv_c_b, ln_c_g, ln_c_b,
              q_norm_g, k_norm_g, attn_sinks, w_out, norm_ffn_g, w_gate, w_up, w_down):
    xp, xs = x_prompt, x_sample
    bp = xp.shape[0]
    new_p = [[], [], [], [], []]
    new_s = [[], [], [], [], []]
    for l in range(DEPTH):
        lw = (norm_mix_g[l], w_in[l], conv_a_w[l], pool_w[l], pool_scale[l], conv_c_w[l], conv_c_b[l],
              ln_c_g[l], ln_c_b[l], q_norm_g[l], k_norm_g[l], attn_sinks[l], w_out[l], norm_ffn_g[l],
              w_gate[l], w_up[l], w_down[l])
        za = jnp.zeros((bp, CONV_A_W - 1, GROUP_W), xp.dtype)
        zp = jnp.zeros((bp, POOL_STATE, GROUP_W), xp.dtype)
        zc = jnp.zeros((bp, CONV_C_W - 1, GROUP_W), xp.dtype)
        xp, sp = trunk_layer(xp, lw, za, zp, zc, 0, None, None, rel_bias)
        xs, ss = trunk_layer(xs, lw, state_conv_a[l], state_pool[l], state_conv_c[l], PAST_LEN,
                             cache_k_win[l], cache_v_win[l], rel_bias)
        for i in range(5):
            new_p[i].append(sp[i])
            new_s[i].append(ss[i])
    return (xp, xs,
            jnp.stack(new_p[0]), jnp.stack(new_p[1]), jnp.stack(new_p[2]), jnp.stack(new_p[3]), jnp.stack(new_p[4]),
            jnp.stack(new_s[0]), jnp.stack(new_s[1]), jnp.stack(new_s[2]), jnp.stack(new_s[3]), jnp.stack(new_s[4]))
```

```python
import functools
import math

import numpy as np
import jax
import jax.numpy as jnp
from jax import lax
from jax.experimental import pallas as pl
from jax.experimental.pallas import tpu as pltpu

F32 = jnp.float32
BF16 = jnp.bfloat16

GROUP_W = 1024
HEAD_DIM = 64
N_HEADS = GROUP_W // HEAD_DIM
N_KV = 4
Q_PER_KV = N_HEADS // N_KV
KV_W = N_KV * HEAD_DIM
WINDOW = 128
CONV_A_W = 3
CONV_C_W = 31
POOL_WINDOWS = (2, 4, 8, 16)
POOL_GROUP_W = GROUP_W // len(POOL_WINDOWS)
POOL_STATE = max(POOL_WINDOWS) - 1
NUM_BUCKETS = 32
MAX_DISTANCE = 128
EPS = 1e-6
NEG_INF = float("-inf")

HALO = 32
NORM_ROWS = 16
VMEM_LIMIT = 56 << 20


def _params(*sem):
    return pltpu.CompilerParams(dimension_semantics=sem, vmem_limit_bytes=VMEM_LIMIT)


def _rmsnorm_to(x_ref, g_ref, h_ref):
    rows = x_ref.shape[0]
    step = min(NORM_ROWS, rows)

    def body(i, carry):
        r = pl.multiple_of(i * step, step)
        x = x_ref[pl.ds(r, step), :]
        ms = jnp.mean(x * x, axis=-1, keepdims=True)
        h_ref[pl.ds(r, step), :] = (x * lax.rsqrt(ms + EPS) * g_ref[...]).astype(h_ref.dtype)
        return carry

    lax.fori_loop(0, rows // step, body, 0)


def _norm_mm_kernel(x_ref, g_ref, w_ref, o_ref, h_ref):
    @pl.when(pl.program_id(1) == 0)
    def _():
        _rmsnorm_to(x_ref, g_ref, h_ref)

    o_ref[...] = jnp.dot(h_ref[...], w_ref[...], preferred_element_type=F32).astype(o_ref.dtype)


def norm_matmul(x, g, w, layer, *, tm, tn):
    m, d = x.shape
    n = w.shape[2]
    assert m % tm == 0 and n % tn == 0
    return pl.pallas_call(
        _norm_mm_kernel,
        out_shape=jax.ShapeDtypeStruct((m, n), F32),
        grid=(m // tm, n // tn),
        in_specs=[
            pl.BlockSpec((tm, d), lambda i, j: (i, 0)),
            pl.BlockSpec((1, d), lambda i, j: (0, 0)),
            pl.BlockSpec((None, d, tn), lambda i, j: (layer, 0, j)),
        ],
        out_specs=pl.BlockSpec((tm, tn), lambda i, j: (i, j)),
        scratch_shapes=[pltpu.VMEM((tm, d), BF16)],
        compiler_params=_params("parallel", "arbitrary"),
        name="norm_in_proj",
    )(x, g, w)


def _norm_gateup_kernel(x_ref, g_ref, wg_ref, wu_ref, o_ref, h_ref):
    @pl.when(pl.program_id(1) == 0)
    def _():
        _rmsnorm_to(x_ref, g_ref, h_ref)

    h = h_ref[...]
    a = jnp.dot(h, wg_ref[...], preferred_element_type=F32)
    b = jnp.dot(h, wu_ref[...], preferred_element_type=F32)
    o_ref[...] = (a * jax.nn.sigmoid(a) * b).astype(o_ref.dtype)


def norm_gateup(x, g, w_gate, w_up, layer, *, tm, tn):
    m, d = x.shape
    f = w_gate.shape[2]
    assert m % tm == 0 and f % tn == 0
    return pl.pallas_call(
        _norm_gateup_kernel,
        out_shape=jax.ShapeDtypeStruct((m, f), BF16),
        grid=(m // tm, f // tn),
        in_specs=[
            pl.BlockSpec((tm, d), lambda i, j: (i, 0)),
            pl.BlockSpec((1, d), lambda i, j: (0, 0)),
            pl.BlockSpec((None, d, tn), lambda i, j: (layer, 0, j)),
            pl.BlockSpec((None, d, tn), lambda i, j: (layer, 0, j)),
        ],
        out_specs=pl.BlockSpec((tm, tn), lambda i, j: (i, j)),
        scratch_shapes=[pltpu.VMEM((tm, d), BF16)],
        compiler_params=_params("parallel", "arbitrary"),
        name="norm_gate_up",
    )(x, g, w_gate, w_up)


def _resid_mm_kernel(*refs, k_splits):
    a_refs = refs[:len(k_splits)]
    w_ref, r_ref, o_ref = refs[len(k_splits):]
    acc = r_ref[...]
    off = 0
    for a_ref, k in zip(a_refs, k_splits):
        acc = acc + jnp.dot(a_ref[...], w_ref[off:off + k, :], preferred_element_type=F32)
        off += k
    o_ref[...] = acc


def resid_matmul(acts, w, layer, resid, *, tm, tn):
    m, n = resid.shape
    k_splits = tuple(a.shape[1] for a in acts)
    k_total = sum(k_splits)
    assert w.shape[1] == k_total and m % tm == 0 and n % tn == 0
    return pl.pallas_call(
        functools.partial(_resid_mm_kernel, k_splits=k_splits),
        out_shape=jax.ShapeDtypeStruct((m, n), F32),
        grid=(m // tm, n // tn),
        in_specs=[pl.BlockSpec((tm, k), lambda i, j: (i, 0)) for k in k_splits] + [
            pl.BlockSpec((None, k_total, tn), lambda i, j: (layer, 0, j)),
            pl.BlockSpec((tm, tn), lambda i, j: (i, j)),
        ],
        out_specs=pl.BlockSpec((tm, tn), lambda i, j: (i, j)),
        compiler_params=_params("parallel", "arbitrary"),
        name="proj_residual",
    )(*acts, w, resid)


def _bucket_matrix():
    i = np.arange(WINDOW)[:, None]
    j = np.arange(2 * WINDOW)[None, :]
    rel = i + WINDOW - j
    max_exact = NUM_BUCKETS // 2
    nf = np.maximum(rel, 1).astype(np.float32)
    large = max_exact + (np.log(nf / max_exact) / math.log(MAX_DISTANCE / max_exact)
                         * (NUM_BUCKETS - max_exact)).astype(np.int32)
    large = np.minimum(large, NUM_BUCKETS - 1)
    bucket = np.where(rel < max_exact, rel, large)
    valid = (rel >= 0) & (rel < WINDOW)
    return np.where(valid, bucket, -1).astype(np.int32)


def _bias_kernel(rb_ref, bk_ref, o_ref):
    h = pl.program_id(0)
    bk = bk_ref[...]
    acc = jnp.full(bk.shape, NEG_INF, F32)
    for b in range(NUM_BUCKETS):
        acc = jnp.where(bk == b, rb_ref[b, h], acc)
    o_ref[...] = acc


def window_bias(rel_bias):
    bucket = jnp.asarray(_bucket_matrix())
    return pl.pallas_call(
        _bias_kernel,
        out_shape=jax.ShapeDtypeStruct((N_HEADS, WINDOW, 2 * WINDOW), F32),
        grid=(N_HEADS,),
        in_specs=[
            pl.BlockSpec(memory_space=pltpu.SMEM),
            pl.BlockSpec((WINDOW, 2 * WINDOW), lambda h: (0, 0)),
        ],
        out_specs=pl.BlockSpec((None, WINDOW, 2 * WINDOW), lambda h: (h, 0, 0)),
        compiler_params=_params("arbitrary"),
        name="window_bias",
    )(rel_bias, bucket)


def _layer_norm_silu(y, g, b):
    mu = jnp.mean(y, axis=-1, keepdims=True)
    var = jnp.mean(jnp.square(y - mu), axis=-1, keepdims=True)
    y = (y - mu) * lax.rsqrt(var + EPS) * g + b
    return y * jax.nn.sigmoid(y)


def _mix_abc_kernel(ha_ref, ba_ref, ca_ref, ub_ref, ac_ref, gc_ref,
                    wa_ref, wp_ref, ps_ref, wc_ref, cb_ref, lg_ref, lb_ref,
                    y_ref, sa_ref, sp_ref, sc_ref,
                    ea_ref, eb_ref, ec_ref, *, tb):
    t = pl.program_id(1)
    nt = pl.num_programs(1)

    @pl.when(t == 0)
    def _():
        zeros = jnp.zeros((HALO, GROUP_W), F32)
        ea_ref[0:HALO, :] = zeros
        eb_ref[0:HALO, :] = zeros
        ec_ref[0:HALO, :] = zeros

    ea_ref[HALO:HALO + tb, :] = ca_ref[...] * ha_ref[...]
    conv = wa_ref[0:1, :] * ea_ref[HALO - 2:HALO - 2 + tb, :]
    conv = conv + wa_ref[1:2, :] * ea_ref[HALO - 1:HALO - 1 + tb, :]
    conv = conv + wa_ref[2:3, :] * ea_ref[HALO:HALO + tb, :]
    y_ref[:, 0:GROUP_W] = (ba_ref[...] * conv).astype(y_ref.dtype)

    eb_ref[HALO:HALO + tb, :] = ub_ref[...]
    pos = t * tb + lax.broadcasted_iota(jnp.int32, (tb, 1), 0)
    for gi, win in enumerate(POOL_WINDOWS):
        lanes = slice(gi * POOL_GROUP_W, (gi + 1) * POOL_GROUP_W)
        s = eb_ref[HALO:HALO + tb, lanes]
        for j in range(1, win):
            s = s + eb_ref[HALO - j:HALO - j + tb, lanes]
        cnt = jnp.minimum(pos + 1, win).astype(F32)
        d = s / cnt - eb_ref[HALO:HALO + tb, lanes]
        yb = jnp.dot(d.astype(BF16), wp_ref[gi], preferred_element_type=F32) * ps_ref[:, lanes]
        y_ref[:, GROUP_W + gi * POOL_GROUP_W:GROUP_W + (gi + 1) * POOL_GROUP_W] = yb.astype(y_ref.dtype)

    ec_ref[HALO:HALO + tb, :] = ac_ref[...] * jax.nn.sigmoid(gc_ref[...])
    base = HALO - (CONV_C_W - 1)
    acc = wc_ref[0:1, :] * ec_ref[base:base + tb, :]
    for k in range(1, CONV_C_W):
        acc = acc + wc_ref[k:k + 1, :] * ec_ref[base + k:base + k + tb, :]
    yc = _layer_norm_silu(acc + cb_ref[...], lg_ref[...], lb_ref[...])
    y_ref[:, 2 * GROUP_W:3 * GROUP_W] = yc.astype(y_ref.dtype)

    @pl.when(t == nt - 1)
    def _():
        end = HALO + tb
        sa_ref[...] = ea_ref[end - (CONV_A_W - 1):end, :]
        sp_ref[...] = eb_ref[end - POOL_STATE:end, :]
        sc_ref[...] = ec_ref[end - (CONV_C_W - 1):end, :]

    ea_ref[0:HALO, :] = ea_ref[tb:tb + HALO, :]
    eb_ref[0:HALO, :] = eb_ref[tb:tb + HALO, :]
    ec_ref[0:HALO, :] = ec_ref[tb:tb + HALO, :]


def mix_abc_prompt(z, bsz, seq, conv_a_w, pool_w, pool_scale, conv_c_w, conv_c_b, ln_g, ln_b, *, tb):
    assert seq % tb == 0 and tb >= HALO
    nt = seq // tb
    col = lambda c: pl.BlockSpec((tb, GROUP_W), lambda b, t: (b * nt + t, c))
    full = lambda a: pl.BlockSpec(a.shape, lambda b, t: (0,) * a.ndim)
    state = lambda rows: pl.BlockSpec((None, rows, GROUP_W), lambda b, t: (b, 0, 0))
    ext = pltpu.VMEM((HALO + tb, GROUP_W), F32)
    return pl.pallas_call(
        functools.partial(_mix_abc_kernel, tb=tb),
        out_shape=(
            jax.ShapeDtypeStruct((bsz * seq, 3 * GROUP_W), BF16),
            jax.ShapeDtypeStruct((bsz, CONV_A_W - 1, GROUP_W), F32),
            jax.ShapeDtypeStruct((bsz, POOL_STATE, GROUP_W), F32),
            jax.ShapeDtypeStruct((bsz, CONV_C_W - 1, GROUP_W), F32),
        ),
        grid=(bsz, nt),
        in_specs=[col(c) for c in range(6)] + [
            full(conv_a_w), full(pool_w), full(pool_scale), full(conv_c_w),
            full(conv_c_b), full(ln_g), full(ln_b)],
        out_specs=(
            pl.BlockSpec((tb, 3 * GROUP_W), lambda b, t: (b * nt + t, 0)),
            state(CONV_A_W - 1), state(POOL_STATE), state(CONV_C_W - 1),
        ),
        scratch_shapes=[ext, ext, ext],
        compiler_params=_params("parallel", "arbitrary"),
        name="mix_abc_prompt",
    )(z, z, z, z, z, z, conv_a_w, pool_w, pool_scale, conv_c_w, conv_c_b, ln_g, ln_b)


def _head_rmsnorm(x, g, scale=1.0):
    ms = jnp.mean(x * x, axis=-1, keepdims=True)
    return x * (lax.rsqrt(ms + EPS) * scale) * g


def _softmax_with_sink(s, sink):
    m = jnp.maximum(jnp.max(s, axis=-1, keepdims=True), sink)
    p = jnp.exp(s - m)
    denom = jnp.sum(p, axis=-1, keepdims=True) + jnp.exp(sink - m)
    return p / denom


def _attn_prompt_kernel(sink_ref, q_ref, kv_ref, qg_ref, kg_ref, bias_ref,
                        y_ref, ko_ref, vo_ref, kp_ref, vp_ref):
    n = pl.program_id(1)

    @pl.when(n == 0)
    def _():
        kp_ref[...] = jnp.zeros_like(kp_ref)
        vp_ref[...] = jnp.zeros_like(vp_ref)

    col = lax.broadcasted_iota(jnp.int32, (Q_PER_KV * WINDOW, 2 * WINDOW), 1)
    before_start = jnp.logical_and(n == 0, col < WINDOW)
    kn_heads = []
    for g in range(N_KV):
        lanes = slice(g * HEAD_DIM, (g + 1) * HEAD_DIM)
        kn = _head_rmsnorm(kv_ref[:, g * HEAD_DIM:(g + 1) * HEAD_DIM], kg_ref[...])
        kn_heads.append(kn)
        v = kv_ref[:, KV_W + g * HEAD_DIM:KV_W + (g + 1) * HEAD_DIM]
        kcat = jnp.concatenate([kp_ref[:, lanes], kn.astype(BF16)], axis=0)
        vcat = jnp.concatenate([vp_ref[:, lanes], v.astype(BF16)], axis=0)
        qs, sinks = [], []
        for j in range(Q_PER_KV):
            h = g * Q_PER_KV + j
            qh = _head_rmsnorm(q_ref[:, h * HEAD_DIM:(h + 1) * HEAD_DIM], qg_ref[...], HEAD_DIM ** -0.5)
            qs.append(qh.astype(BF16))
            sinks.append(jnp.full((WINDOW, 1), sink_ref[h], F32))
        qcat = jnp.concatenate(qs, axis=0)
        sink = jnp.concatenate(sinks, axis=0)
        s = lax.dot_general(qcat, kcat, (((1,), (1,)), ((), ())), preferred_element_type=F32)
        s = s + bias_ref[g * Q_PER_KV:(g + 1) * Q_PER_KV].reshape(Q_PER_KV * WINDOW, 2 * WINDOW)
        s = jnp.where(before_start, NEG_INF, s)
        p = _softmax_with_sink(s, sink)
        o = jnp.dot(p.astype(BF16), vcat, preferred_element_type=F32)
        for j in range(Q_PER_KV):
            h = g * Q_PER_KV + j
            y_ref[:, h * HEAD_DIM:(h + 1) * HEAD_DIM] = o[j * WINDOW:(j + 1) * WINDOW].astype(y_ref.dtype)

    kn_all = jnp.concatenate(kn_heads, axis=1)
    v_all = kv_ref[:, KV_W:2 * KV_W]
    ko_ref[...] = kn_all
    vo_ref[...] = v_all
    kp_ref[...] = kn_all.astype(BF16)
    vp_ref[...] = v_all.astype(BF16)


def attn_prompt(z, bsz, seq, q_norm_g, k_norm_g, sinks, bias):
    assert seq % WINDOW == 0
    nb = seq // WINDOW
    q_col = 6 * GROUP_W // GROUP_W
    kv_col = 7 * GROUP_W // (2 * KV_W)
    win = pl.BlockSpec((None, WINDOW, KV_W), lambda b, n: (b, 0, 0))
    return pl.pallas_call(
        _attn_prompt_kernel,
        out_shape=(
            jax.ShapeDtypeStruct((bsz * seq, GROUP_W), BF16),
            jax.ShapeDtypeStruct((bsz, WINDOW, KV_W), F32),
            jax.ShapeDtypeStruct((bsz, WINDOW, KV_W), F32),
        ),
        grid=(bsz, nb),
        in_specs=[
            pl.BlockSpec(memory_space=pltpu.SMEM),
            pl.BlockSpec((WINDOW, GROUP_W), lambda b, n: (b * nb + n, q_col)),
            pl.BlockSpec((WINDOW, 2 * KV_W), lambda b, n: (b * nb + n, kv_col)),
            pl.BlockSpec((1, HEAD_DIM), lambda b, n: (0, 0)),
            pl.BlockSpec((1, HEAD_DIM), lambda b, n: (0, 0)),
            pl.BlockSpec((N_HEADS, WINDOW, 2 * WINDOW), lambda b, n: (0, 0, 0)),
        ],
        out_specs=(pl.BlockSpec((WINDOW, GROUP_W), lambda b, n: (b * nb + n, 0)), win, win),
        scratch_shapes=[pltpu.VMEM((WINDOW, KV_W), BF16), pltpu.VMEM((WINDOW, KV_W), BF16)],
        compiler_params=_params("parallel", "arbitrary"),
        name="attn_prompt",
    )(sinks, z, z, q_norm_g, k_norm_g, bias)


def _mix_sample_kernel(sink_ref, z_ref, sa_ref, sp_ref, sc_ref, kb_ref, vb_ref,
                       wa_ref, wp_ref, ps_ref, wc_ref, cb_ref, lg_ref, lb_ref, qg_ref, kg_ref, bias_ref,
                       y_ref, sao_ref, spo_ref, sco_ref, ko_ref, vo_ref, *, start_pos):
    gw = GROUP_W
    zc = lambda c, w=gw: z_ref[:, c:c + w]

    ua = zc(2 * gw) * zc(0)
    conv = wa_ref[0:1, :] * sa_ref[0:1, :] + wa_ref[1:2, :] * sa_ref[1:2, :] + wa_ref[2:3, :] * ua
    y_ref[:, 0:gw] = (zc(gw) * conv).astype(y_ref.dtype)
    sao_ref[0:1, :] = sa_ref[1:2, :]
    sao_ref[1:2, :] = ua

    ub = zc(3 * gw)
    for gi, win in enumerate(POOL_WINDOWS):
        lanes = slice(gi * POOL_GROUP_W, (gi + 1) * POOL_GROUP_W)
        u = ub[:, lanes]
        s = u + jnp.sum(sp_ref[POOL_STATE - (win - 1):POOL_STATE, lanes], axis=0, keepdims=True)
        cnt = float(min(start_pos + 1, win))
        d = s / cnt - u
        yb = jnp.dot(d.astype(BF16), wp_ref[gi], preferred_element_type=F32) * ps_ref[:, lanes]
        y_ref[:, gw + gi * POOL_GROUP_W:gw + (gi + 1) * POOL_GROUP_W] = yb.astype(y_ref.dtype)
    spo_ref[0:POOL_STATE - 1, :] = sp_ref[1:POOL_STATE, :]
    spo_ref[POOL_STATE - 1:POOL_STATE, :] = ub

    uc = zc(4 * gw) * jax.nn.sigmoid(zc(5 * gw))
    kc = CONV_C_W - 1
    acc = jnp.sum(wc_ref[0:kc, :] * sc_ref[...], axis=0, keepdims=True) + wc_ref[kc:kc + 1, :] * uc
    yc = _layer_norm_silu(acc + cb_ref[...], lg_ref[...], lb_ref[...])
    y_ref[:, 2 * gw:3 * gw] = yc.astype(y_ref.dtype)
    sco_ref[0:kc - 1, :] = sc_ref[1:kc, :]
    sco_ref[kc - 1:kc, :] = uc

    wb = kb_ref.shape[0]
    for g in range(N_KV):
        lanes = slice(g * HEAD_DIM, (g + 1) * HEAD_DIM)
        kn = _head_rmsnorm(zc(7 * gw + g * HEAD_DIM, HEAD_DIM), kg_ref[...])
        vn = zc(7 * gw + KV_W + g * HEAD_DIM, HEAD_DIM)
        ko_ref[wb - 1:wb, lanes] = kn
        vo_ref[wb - 1:wb, lanes] = vn
        qs, sinks = [], []
        for j in range(Q_PER_KV):
            h = g * Q_PER_KV + j
            qs.append(_head_rmsnorm(zc(6 * gw + h * HEAD_DIM, HEAD_DIM), qg_ref[...], HEAD_DIM ** -0.5))
            sinks.append(jnp.full((1, 1), sink_ref[h], F32))
        qcat = jnp.concatenate(qs, axis=0)
        sink = jnp.concatenate(sinks, axis=0)
        hb = bias_ref[g * Q_PER_KV:(g + 1) * Q_PER_KV, :]
        s_buf = lax.dot_general(qcat.astype(BF16), kb_ref[:, lanes].astype(BF16), (((1,), (1,)), ((), ())),
                                preferred_element_type=F32) + hb[:, WINDOW - wb:WINDOW]
        s_new = jnp.sum(qcat * kn, axis=-1, keepdims=True) + hb[:, WINDOW:WINDOW + 1]
        m = jnp.maximum(jnp.maximum(jnp.max(s_buf, axis=-1, keepdims=True), s_new), sink)
        p_buf = jnp.exp(s_buf - m)
        p_new = jnp.exp(s_new - m)
        denom = jnp.sum(p_buf, axis=-1, keepdims=True) + p_new + jnp.exp(sink - m)
        o = jnp.dot((p_buf / denom).astype(BF16), vb_ref[:, lanes].astype(BF16), preferred_element_type=F32)
        o = o + (p_new / denom) * vn
        for j in range(Q_PER_KV):
            h = g * Q_PER_KV + j
            y_ref[:, 3 * gw + h * HEAD_DIM:3 * gw + (h + 1) * HEAD_DIM] = o[j:j + 1, :].astype(y_ref.dtype)
    ko_ref[0:wb - 1, :] = kb_ref[1:wb, :]
    vo_ref[0:wb - 1, :] = vb_ref[1:wb, :]


def mix_sample(z, sa, sp, sc, kb, vb, start_pos, conv_a_w, pool_w, pool_scale, conv_c_w, conv_c_b, ln_g, ln_b,
               q_norm_g, k_norm_g, sinks, bias_row):
    nseq, in_cols = z.shape
    wb = kb.shape[1]
    assert wb == WINDOW
    per_seq = lambda a: pl.BlockSpec((None,) + a.shape[1:], lambda s: (s,) + (0,) * (a.ndim - 1))
    full = lambda a: pl.BlockSpec(a.shape, lambda s: (0,) * a.ndim)
    z3 = z.reshape(nseq, 1, in_cols)
    outs = pl.pallas_call(
        functools.partial(_mix_sample_kernel, start_pos=start_pos),
        out_shape=(
            jax.ShapeDtypeStruct((nseq, 1, 4 * GROUP_W), BF16),
            jax.ShapeDtypeStruct(sa.shape, F32), jax.ShapeDtypeStruct(sp.shape, F32),
            jax.ShapeDtypeStruct(sc.shape, F32), jax.ShapeDtypeStruct(kb.shape, F32),
            jax.ShapeDtypeStruct(vb.shape, F32),
        ),
        grid=(nseq,),
        in_specs=[pl.BlockSpec(memory_space=pltpu.SMEM)] + [per_seq(a) for a in (z3, sa, sp, sc, kb, vb)] + [
            full(a) for a in (conv_a_w, pool_w, pool_scale, conv_c_w, conv_c_b, ln_g, ln_b,
                              q_norm_g, k_norm_g, bias_row)],
        out_specs=tuple(per_seq(a) for a in (jax.ShapeDtypeStruct((nseq, 1, 4 * GROUP_W), BF16), sa, sp, sc, kb, vb)),
        compiler_params=_params("parallel"),
        name="mix_sample",
    )(sinks, z3, sa, sp, sc, kb, vb, conv_a_w, pool_w, pool_scale, conv_c_w, conv_c_b, ln_g, ln_b,
      q_norm_g, k_norm_g, bias_row)
    return (outs[0].reshape(nseq, 4 * GROUP_W),) + tuple(outs[1:])


def kernel(x_prompt, x_sample, state_conv_a, state_pool, state_conv_c, cache_k_win, cache_v_win, rel_bias,
           norm_mix_g, w_in, conv_a_w, pool_w, pool_scale, conv_c_w, conv_c_b, ln_c_g, ln_c_b, q_norm_g, k_norm_g,
           attn_sinks, w_out, norm_ffn_g, w_gate, w_up, w_down):
    bsz, seq, d_model = x_prompt.shape
    nseq = x_sample.shape[0]
    depth = w_in.shape[0]
    past_len = 8192
    wb = cache_k_win.shape[2]

    w_in_b, w_out_b, w_gate_b, w_up_b, w_down_b, pool_w_b = (
        a.astype(BF16) for a in (w_in, w_out, w_gate, w_up, w_down, pool_w))
    bias = window_bias(rel_bias)
    bias_row = bias[:, 0, :]

    xp = x_prompt.reshape(bsz * seq, d_model)
    xs = x_sample.reshape(nseq, d_model)
    kbuf = cache_k_win.reshape(depth, nseq, wb, KV_W)
    vbuf = cache_v_win.reshape(depth, nseq, wb, KV_W)
    row = lambda a, l: a[l][None, :]

    new_p = [[] for _ in range(5)]
    new_s = [[] for _ in range(5)]
    for l in range(depth):
        mixer_w = (conv_a_w[l], pool_w_b[l], row(pool_scale, l), conv_c_w[l], row(conv_c_b, l),
                   row(ln_c_g, l), row(ln_c_b, l))
        qg, kg = row(q_norm_g, l), row(k_norm_g, l)

        zp = norm_matmul(xp, row(norm_mix_g, l), w_in_b, l, tm=512, tn=768)
        y_abc, sa_p, sp_p, sc_p = mix_abc_prompt(zp, bsz, seq, *mixer_w, tb=256)
        y_d, k_p, v_p = attn_prompt(zp, bsz, seq, qg, kg, attn_sinks[l], bias)
        xp = resid_matmul((y_abc, y_d), w_out_b, l, xp, tm=512, tn=1024)
        ff = norm_gateup(xp, row(norm_ffn_g, l), w_gate_b, w_up_b, l, tm=512, tn=256)
        xp = resid_matmul((ff,), w_down_b, l, xp, tm=512, tn=256)
        for i, a in enumerate((sa_p, sp_p, sc_p, k_p.reshape(bsz, WINDOW, N_KV, HEAD_DIM),
                               v_p.reshape(bsz, WINDOW, N_KV, HEAD_DIM))):
            new_p[i].append(a)

        zs = norm_matmul(xs, row(norm_mix_g, l), w_in_b, l, tm=nseq, tn=1536)
        y_s, sa_s, sp_s, sc_s, k_s, v_s = mix_sample(
            zs, state_conv_a[l], state_pool[l], state_conv_c[l], kbuf[l], vbuf[l], past_len,
            *mixer_w, qg, kg, attn_sinks[l], bias_row)
        xs = resid_matmul((y_s,), w_out_b, l, xs, tm=nseq, tn=1024)
        ffs = norm_gateup(xs, row(norm_ffn_g, l), w_gate_b, w_up_b, l, tm=nseq, tn=256)
        xs = resid_matmul((ffs,), w_down_b, l, xs, tm=nseq, tn=512)
        for i, a in enumerate((sa_s, sp_s, sc_s, k_s.reshape(nseq, wb, N_KV, HEAD_DIM),
                               v_s.reshape(nseq, wb, N_KV, HEAD_DIM))):
            new_s[i].append(a)

    return (xp.reshape(bsz, seq, d_model), xs.reshape(nseq, 1, d_model),
            *(jnp.stack(a) for a in new_p), *(jnp.stack(a) for a in new_s))
```

```python
import functools
import math

import numpy as np
import jax
import jax.numpy as jnp
from jax import lax
from jax.experimental import pallas as pl
from jax.experimental.pallas import tpu as pltpu

F32 = jnp.float32
BF16 = jnp.bfloat16

GROUP_W = 1024
HEAD_DIM = 64
N_HEADS = GROUP_W // HEAD_DIM
N_KV = 4
Q_PER_KV = N_HEADS // N_KV
KV_W = N_KV * HEAD_DIM
WINDOW = 128
CONV_A_W = 3
CONV_C_W = 31
POOL_WINDOWS = (2, 4, 8, 16)
POOL_GROUP_W = GROUP_W // len(POOL_WINDOWS)
POOL_STATE = max(POOL_WINDOWS) - 1
NUM_BUCKETS = 32
MAX_DISTANCE = 128
EPS = 1e-6
NEG_INF = float("-inf")

PAST_LEN = 8192

LANES = 128
MXU_COLS = 256
HALO = 32
NORM_ROWS = 16
VMEM_LIMIT = 56 << 20


def _params(*sem):
    return pltpu.CompilerParams(dimension_semantics=sem, vmem_limit_bytes=VMEM_LIMIT)


def _rmsnorm_to(x_ref, g_ref, h_ref):
    rows = x_ref.shape[0]
    step = min(NORM_ROWS, rows)

    def body(i, carry):
        r = pl.multiple_of(i * step, step)
        x = x_ref[pl.ds(r, step), :]
        ms = jnp.mean(x * x, axis=-1, keepdims=True)
        h_ref[pl.ds(r, step), :] = (x * lax.rsqrt(ms + EPS) * g_ref[...]).astype(h_ref.dtype)
        return carry

    lax.fori_loop(0, rows // step, body, 0)


def _norm_mm_cast_kernel(x_ref, g_ref, w_ref, o_ref, wb_ref, h_ref):
    @pl.when(pl.program_id(0) == 0)
    def _():
        _rmsnorm_to(x_ref, g_ref, h_ref)

    wb_ref[...] = w_ref[...].astype(BF16)
    o_ref[...] = jnp.dot(h_ref[...], wb_ref[...], preferred_element_type=F32)


def norm_matmul_cast(x, g, w, layer, *, tn):
    m, d = x.shape
    n = w.shape[2]
    assert n % tn == 0
    return pl.pallas_call(
        _norm_mm_cast_kernel,
        out_shape=(jax.ShapeDtypeStruct((m, n), F32), jax.ShapeDtypeStruct((d, n), BF16)),
        grid=(n // tn,),
        in_specs=[
            pl.BlockSpec((m, d), lambda j: (0, 0)),
            pl.BlockSpec((1, d), lambda j: (0, 0)),
            pl.BlockSpec((None, d, tn), lambda j: (layer, 0, j)),
        ],
        out_specs=(pl.BlockSpec((m, tn), lambda j: (0, j)), pl.BlockSpec((d, tn), lambda j: (0, j))),
        scratch_shapes=[pltpu.VMEM((m, d), BF16)],
        compiler_params=_params("arbitrary"),
        name="sample_in_proj",
    )(x, g, w)


def _norm_gateup_cast_kernel(x_ref, g_ref, wg_ref, wu_ref, o_ref, wgb_ref, wub_ref, h_ref):
    @pl.when(pl.program_id(0) == 0)
    def _():
        _rmsnorm_to(x_ref, g_ref, h_ref)

    wgb_ref[...] = wg_ref[...].astype(BF16)
    wub_ref[...] = wu_ref[...].astype(BF16)
    h = h_ref[...]
    a = jnp.dot(h, wgb_ref[...], preferred_element_type=F32)
    b = jnp.dot(h, wub_ref[...], preferred_element_type=F32)
    o_ref[...] = (a * jax.nn.sigmoid(a) * b).astype(o_ref.dtype)


def norm_gateup_cast(x, g, w_gate, w_up, layer, *, tn):
    m, d = x.shape
    f = w_gate.shape[2]
    assert f % tn == 0
    w_in = pl.BlockSpec((None, d, tn), lambda j: (layer, 0, j))
    w_out = pl.BlockSpec((d, tn), lambda j: (0, j))
    return pl.pallas_call(
        _norm_gateup_cast_kernel,
        out_shape=(jax.ShapeDtypeStruct((m, f), BF16), jax.ShapeDtypeStruct((d, f), BF16),
                   jax.ShapeDtypeStruct((d, f), BF16)),
        grid=(f // tn,),
        in_specs=[pl.BlockSpec((m, d), lambda j: (0, 0)), pl.BlockSpec((1, d), lambda j: (0, 0)), w_in, w_in],
        out_specs=(pl.BlockSpec((m, tn), lambda j: (0, j)), w_out, w_out),
        scratch_shapes=[pltpu.VMEM((m, d), BF16)],
        compiler_params=_params("arbitrary"),
        name="sample_gate_up",
    )(x, g, w_gate, w_up)


def _resid_mm_cast_kernel(a_ref, w_ref, r_ref, o_ref, wb_ref):
    wb_ref[...] = w_ref[...].astype(BF16)
    o_ref[...] = r_ref[...] + jnp.dot(a_ref[...], wb_ref[...], preferred_element_type=F32)


def resid_matmul_cast(a, w, layer, resid, *, tn):
    m, n = resid.shape
    k = a.shape[1]
    assert w.shape[1] == k and n % tn == 0
    return pl.pallas_call(
        _resid_mm_cast_kernel,
        out_shape=(jax.ShapeDtypeStruct((m, n), F32), jax.ShapeDtypeStruct((k, n), BF16)),
        grid=(n // tn,),
        in_specs=[
            pl.BlockSpec((m, k), lambda j: (0, 0)),
            pl.BlockSpec((None, k, tn), lambda j: (layer, 0, j)),
            pl.BlockSpec((m, tn), lambda j: (0, j)),
        ],
        out_specs=(pl.BlockSpec((m, tn), lambda j: (0, j)), pl.BlockSpec((k, tn), lambda j: (0, j))),
        compiler_params=_params("arbitrary"),
        name="sample_proj_residual",
    )(a, w, resid)


def _sum_lane_groups(x):
    acc = x[:, 0:LANES]
    for k in range(1, x.shape[1] // LANES):
        acc = acc + x[:, k * LANES:(k + 1) * LANES]
    return acc


def _rowscale_to(ss_ref, rs_ref, d):
    tot = jnp.sum(_sum_lane_groups(ss_ref[...]), axis=-1, keepdims=True)
    rs_ref[...] = jnp.broadcast_to(lax.rsqrt(tot / d + EPS), rs_ref.shape)


def _norm_prep_kernel(x_ref, g_ref, hb_ref, ss_ref):
    rows = x_ref.shape[0]

    def body(i, carry):
        r = pl.multiple_of(i * NORM_ROWS, NORM_ROWS)
        x = x_ref[pl.ds(r, NORM_ROWS), :]
        hb_ref[pl.ds(r, NORM_ROWS), :] = (x * g_ref[...]).astype(BF16)
        ss_ref[pl.ds(r, NORM_ROWS), :] = _sum_lane_groups(x * x)
        return carry

    lax.fori_loop(0, rows // NORM_ROWS, body, 0)


def norm_prep(x, g, *, tm):
    m, d = x.shape
    assert m % tm == 0 and tm % NORM_ROWS == 0
    return pl.pallas_call(
        _norm_prep_kernel,
        out_shape=(jax.ShapeDtypeStruct((m, d), BF16), jax.ShapeDtypeStruct((m, LANES), F32)),
        grid=(m // tm,),
        in_specs=[pl.BlockSpec((tm, d), lambda i: (i, 0)), pl.BlockSpec((1, d), lambda i: (0, 0))],
        out_specs=(pl.BlockSpec((tm, d), lambda i: (i, 0)), pl.BlockSpec((tm, LANES), lambda i: (i, 0))),
        compiler_params=_params("parallel"),
        name="norm_prep",
    )(x, g)


def _scaled_mm_kernel(hb_ref, ss_ref, w_ref, o_ref, rs_ref):
    @pl.when(pl.program_id(1) == 0)
    def _():
        _rowscale_to(ss_ref, rs_ref, hb_ref.shape[1])

    acc = jnp.dot(hb_ref[...], w_ref[...], preferred_element_type=F32)
    rs = rs_ref[...]
    for k in range(o_ref.shape[1] // LANES):
        o_ref[:, k * LANES:(k + 1) * LANES] = acc[:, k * LANES:(k + 1) * LANES] * rs


def scaled_matmul(hb, ss, w, *, tm, tn):
    m, d = hb.shape
    n = w.shape[1]
    assert m % tm == 0 and n % tn == 0 and tn % LANES == 0
    return pl.pallas_call(
        _scaled_mm_kernel,
        out_shape=jax.ShapeDtypeStruct((m, n), F32),
        grid=(m // tm, n // tn),
        in_specs=[
            pl.BlockSpec((tm, d), lambda i, j: (i, 0)),
            pl.BlockSpec((tm, ss.shape[1]), lambda i, j: (i, 0)),
            pl.BlockSpec((d, tn), lambda i, j: (0, j)),
        ],
        out_specs=pl.BlockSpec((tm, tn), lambda i, j: (i, j)),
        scratch_shapes=[pltpu.VMEM((tm, LANES), F32)],
        compiler_params=_params("parallel", "arbitrary"),
        name="in_proj",
    )(hb, ss, w)


def _scaled_gateup_kernel(hb_ref, ss_ref, wg_ref, wu_ref, o_ref, rs_ref):
    @pl.when(pl.program_id(1) == 0)
    def _():
        _rowscale_to(ss_ref, rs_ref, hb_ref.shape[1])

    half = hb_ref.shape[0] // 2
    for r0 in (0, half):
        rows = slice(r0, r0 + half)
        h = hb_ref[rows, :]
        a = jnp.dot(h, wg_ref[...], preferred_element_type=F32)
        b = jnp.dot(h, wu_ref[...], preferred_element_type=F32)
        rs = rs_ref[rows, :]
        for k in range(o_ref.shape[1] // LANES):
            cols = slice(k * LANES, (k + 1) * LANES)
            ak = a[:, cols] * rs
            o_ref[rows, cols] = (ak * jax.nn.sigmoid(ak) * (b[:, cols] * rs)).astype(o_ref.dtype)


def scaled_gateup(hb, ss, w_gate, w_up, *, tm, tn):
    m, d = hb.shape
    f = w_gate.shape[1]
    assert m % tm == 0 and f % tn == 0 and tn % LANES == 0
    w_spec = pl.BlockSpec((d, tn), lambda i, j: (0, j))
    return pl.pallas_call(
        _scaled_gateup_kernel,
        out_shape=jax.ShapeDtypeStruct((m, f), BF16),
        grid=(m // tm, f // tn),
        in_specs=[pl.BlockSpec((tm, d), lambda i, j: (i, 0)), pl.BlockSpec((tm, ss.shape[1]), lambda i, j: (i, 0)),
                  w_spec, w_spec],
        out_specs=pl.BlockSpec((tm, tn), lambda i, j: (i, j)),
        scratch_shapes=[pltpu.VMEM((tm, LANES), F32)],
        compiler_params=_params("parallel", "arbitrary"),
        name="gate_up",
    )(hb, ss, w_gate, w_up)


def _resid_mm_kernel(*refs, k_splits, emit_norm):
    n_a = len(k_splits)
    a_refs = refs[:n_a]
    if emit_norm:
        w_ref, r_ref, g_ref, o_ref, hb_ref, ss_ref = refs[n_a:]
    else:
        w_ref, r_ref, o_ref = refs[n_a:]
    tn = o_ref.shape[1]
    part = None
    for c in range(0, tn, MXU_COLS):
        cols = slice(c, c + MXU_COLS)
        acc = r_ref[:, cols]
        off = 0
        for a_ref, k in zip(a_refs, k_splits):
            acc = acc + jnp.dot(a_ref[...], w_ref[off:off + k, cols], preferred_element_type=F32)
            off += k
        o_ref[:, cols] = acc
        if emit_norm:
            hb_ref[:, cols] = (acc * g_ref[:, cols]).astype(BF16)
            ss = _sum_lane_groups(acc * acc)
            part = ss if part is None else part + ss
    if emit_norm:
        ss_ref[...] = part


def resid_matmul(acts, w, resid, g_next=None, *, tm, tn):
    m, n = resid.shape
    k_splits = tuple(a.shape[1] for a in acts)
    k_total = sum(k_splits)
    emit_norm = g_next is not None
    assert w.shape[0] == k_total and m % tm == 0 and n % tn == 0 and tn % MXU_COLS == 0
    tile = pl.BlockSpec((tm, tn), lambda i, j: (i, j))
    in_specs = [pl.BlockSpec((tm, k), lambda i, j: (i, 0)) for k in k_splits] + [
        pl.BlockSpec((k_total, tn), lambda i, j: (0, j)), tile]
    out_shape = jax.ShapeDtypeStruct((m, n), F32)
    out_specs = tile
    args = (*acts, w, resid)
    if emit_norm:
        in_specs.append(pl.BlockSpec((1, tn), lambda i, j: (0, j)))
        out_shape = (out_shape, jax.ShapeDtypeStruct((m, n), BF16),
                     jax.ShapeDtypeStruct((m, (n // tn) * LANES), F32))
        out_specs = (tile, tile, pl.BlockSpec((tm, LANES), lambda i, j: (i, j)))
        args = args + (g_next,)
    return pl.pallas_call(
        functools.partial(_resid_mm_kernel, k_splits=k_splits, emit_norm=emit_norm),
        out_shape=out_shape,
        grid=(m // tm, n // tn),
        in_specs=in_specs,
        out_specs=out_specs,
        compiler_params=_params("parallel", "arbitrary"),
        name="proj_residual",
    )(*args)


def _bucket_matrix():
    i = np.arange(WINDOW)[:, None]
    j = np.arange(2 * WINDOW)[None, :]
    rel = i + WINDOW - j
    max_exact = NUM_BUCKETS // 2
    nf = np.maximum(rel, 1).astype(np.float32)
    large = max_exact + (np.log(nf / max_exact) / math.log(MAX_DISTANCE / max_exact)
                         * (NUM_BUCKETS - max_exact)).astype(np.int32)
    large = np.minimum(large, NUM_BUCKETS - 1)
    bucket = np.where(rel < max_exact, rel, large)
    valid = (rel >= 0) & (rel < WINDOW)
    later = np.where(valid, bucket, -1).astype(np.int32)
    first = np.where(j >= WINDOW, later, -1)
    return np.stack([later, first])


def _bias_kernel(rb_ref, bk_ref, o_ref):
    t = pl.program_id(1)
    per = N_HEADS // N_KV
    h = per * (t // per) + 2 * (t % 2) + (t % per) // 2
    bk = bk_ref[...]
    acc = jnp.full(bk.shape, NEG_INF, F32)
    for b in range(NUM_BUCKETS):
        acc = jnp.where(bk == b, rb_ref[b, h], acc)
    o_ref[...] = acc


def window_bias(rel_bias):
    bucket = jnp.asarray(_bucket_matrix())
    return pl.pallas_call(
        _bias_kernel,
        out_shape=jax.ShapeDtypeStruct((2, N_HEADS, WINDOW, 2 * WINDOW), F32),
        grid=(2, N_HEADS),
        in_specs=[
            pl.BlockSpec(memory_space=pltpu.SMEM),
            pl.BlockSpec((None, WINDOW, 2 * WINDOW), lambda f, h: (f, 0, 0)),
        ],
        out_specs=pl.BlockSpec((None, None, WINDOW, 2 * WINDOW), lambda f, h: (f, h, 0, 0)),
        compiler_params=_params("arbitrary", "arbitrary"),
        name="window_bias",
    )(rel_bias, bucket)


def _layer_norm_silu(y, g, b):
    mu = jnp.mean(y, axis=-1, keepdims=True)
    var = jnp.mean(jnp.square(y - mu), axis=-1, keepdims=True)
    y = (y - mu) * lax.rsqrt(var + EPS) * g + b
    return y * jax.nn.sigmoid(y)


def _mix_abc_kernel(ha_ref, ba_ref, ca_ref, ub_ref, ac_ref, gc_ref,
                    wa_ref, wp_ref, ps_ref, wc_ref, cb_ref, lg_ref, lb_ref,
                    y_ref, sa_ref, sp_ref, sc_ref,
                    ea_ref, eb_ref, ec_ref, *, tb):
    t = pl.program_id(1)
    nt = pl.num_programs(1)

    @pl.when(t == 0)
    def _():
        zeros = jnp.zeros((HALO, GROUP_W), F32)
        ea_ref[0:HALO, :] = zeros
        eb_ref[0:HALO, :] = zeros
        ec_ref[0:HALO, :] = zeros

    ea_ref[HALO:HALO + tb, :] = ca_ref[...] * ha_ref[...]
    conv = wa_ref[0:1, :] * ea_ref[HALO - 2:HALO - 2 + tb, :]
    conv = conv + wa_ref[1:2, :] * ea_ref[HALO - 1:HALO - 1 + tb, :]
    conv = conv + wa_ref[2:3, :] * ea_ref[HALO:HALO + tb, :]
    y_ref[:, 0:GROUP_W] = (ba_ref[...] * conv).astype(y_ref.dtype)

    eb_ref[HALO:HALO + tb, :] = ub_ref[...]
    pos = t * tb + lax.broadcasted_iota(jnp.int32, (tb, 1), 0)
    for gi, win in enumerate(POOL_WINDOWS):
        lanes = slice(gi * POOL_GROUP_W, (gi + 1) * POOL_GROUP_W)
        s = eb_ref[HALO:HALO + tb, lanes]
        for j in range(1, win):
            s = s + eb_ref[HALO - j:HALO - j + tb, lanes]
        cnt = jnp.minimum(pos + 1, win).astype(F32)
        d = s / cnt - eb_ref[HALO:HALO + tb, lanes]
        yb = jnp.dot(d.astype(BF16), wp_ref[gi], preferred_element_type=F32) * ps_ref[:, lanes]
        y_ref[:, GROUP_W + gi * POOL_GROUP_W:GROUP_W + (gi + 1) * POOL_GROUP_W] = yb.astype(y_ref.dtype)

    ec_ref[HALO:HALO + tb, :] = ac_ref[...] * jax.nn.sigmoid(gc_ref[...])
    base = HALO - (CONV_C_W - 1)
    acc = wc_ref[0:1, :] * ec_ref[base:base + tb, :]
    for k in range(1, CONV_C_W):
        acc = acc + wc_ref[k:k + 1, :] * ec_ref[base + k:base + k + tb, :]
    yc = _layer_norm_silu(acc + cb_ref[...], lg_ref[...], lb_ref[...])
    y_ref[:, 2 * GROUP_W:3 * GROUP_W] = yc.astype(y_ref.dtype)

    @pl.when(t == nt - 1)
    def _():
        end = HALO + tb
        sa_ref[...] = ea_ref[end - (CONV_A_W - 1):end, :]
        sp_ref[...] = eb_ref[end - POOL_STATE:end, :]
        sc_ref[...] = ec_ref[end - (CONV_C_W - 1):end, :]

    ea_ref[0:HALO, :] = ea_ref[tb:tb + HALO, :]
    eb_ref[0:HALO, :] = eb_ref[tb:tb + HALO, :]
    ec_ref[0:HALO, :] = ec_ref[tb:tb + HALO, :]


def mix_abc_prompt(z, bsz, seq, conv_a_w, pool_w, pool_scale, conv_c_w, conv_c_b, ln_g, ln_b, *, tb):
    assert seq % tb == 0 and tb >= HALO
    nt = seq // tb
    col = lambda c: pl.BlockSpec((tb, GROUP_W), lambda b, t: (b * nt + t, c))
    full = lambda a: pl.BlockSpec(a.shape, lambda b, t: (0,) * a.ndim)
    state = lambda rows: pl.BlockSpec((None, rows, GROUP_W), lambda b, t: (b, 0, 0))
    ext = pltpu.VMEM((HALO + tb, GROUP_W), F32)
    return pl.pallas_call(
        functools.partial(_mix_abc_kernel, tb=tb),
        out_shape=(
            jax.ShapeDtypeStruct((bsz * seq, 3 * GROUP_W), BF16),
            jax.ShapeDtypeStruct((bsz, CONV_A_W - 1, GROUP_W), F32),
            jax.ShapeDtypeStruct((bsz, POOL_STATE, GROUP_W), F32),
            jax.ShapeDtypeStruct((bsz, CONV_C_W - 1, GROUP_W), F32),
        ),
        grid=(bsz, nt),
        in_specs=[col(c) for c in range(6)] + [
            full(conv_a_w), full(pool_w), full(pool_scale), full(conv_c_w),
            full(conv_c_b), full(ln_g), full(ln_b)],
        out_specs=(
            pl.BlockSpec((tb, 3 * GROUP_W), lambda b, t: (b * nt + t, 0)),
            state(CONV_A_W - 1), state(POOL_STATE), state(CONV_C_W - 1),
        ),
        scratch_shapes=[ext, ext, ext],
        compiler_params=_params("parallel", "arbitrary"),
        name="mix_abc_prompt",
    )(z, z, z, z, z, z, conv_a_w, pool_w, pool_scale, conv_c_w, conv_c_b, ln_g, ln_b)


def _head_rmsnorm(x, g, scale=1.0):
    ms = jnp.mean(x * x, axis=-1, keepdims=True)
    return x * (lax.rsqrt(ms + EPS) * scale) * g


def _headwise_rmsnorm(x, e, g):
    x2 = x * x
    hi = x2.astype(BF16)
    lo = (x2 - hi.astype(F32)).astype(BF16)
    sums = []
    for c in range(0, x.shape[1], MXU_COLS):
        cols = slice(c, c + MXU_COLS)
        sums.append(jnp.dot(hi[:, cols], e, preferred_element_type=F32)
                    + jnp.dot(lo[:, cols], e, preferred_element_type=F32))
    ss = jnp.concatenate(sums, axis=1)
    return x * lax.rsqrt(ss * (1.0 / HEAD_DIM) + EPS) * g


def _attn_prompt_kernel(sink_ref, q_ref, kv_ref, qg_ref, kg_ref, e_ref, bias_ref,
                        y_ref, ko_ref, vo_ref, kp_ref, vp_ref):
    n = pl.program_id(1)

    @pl.when(n == 0)
    def _():
        kp_ref[...] = jnp.zeros_like(kp_ref)
        vp_ref[...] = jnp.zeros_like(vp_ref)

    first = (n == 0).astype(jnp.int32)
    e = e_ref[...]
    qn = _headwise_rmsnorm(q_ref[...], e, qg_ref[...]).astype(BF16)
    kn = _headwise_rmsnorm(kv_ref[:, 0:KV_W], e, kg_ref[...])
    v = kv_ref[:, KV_W:2 * KV_W]
    ko_ref[...] = kn
    vo_ref[...] = v
    knb = kn.astype(BF16)
    vb = v.astype(BF16)

    low_q = lax.broadcasted_iota(jnp.int32, (2 * WINDOW, LANES), 1) < HEAD_DIM
    ones_half = (jnp.where(low_q, 1.0, 0.0).astype(BF16), jnp.where(low_q, 0.0, 1.0).astype(BF16))
    keep_lo, keep_hi = ones_half
    for g in range(N_KV):
        lanes = slice((g // 2) * LANES, (g // 2 + 1) * LANES)
        k_slab = jnp.concatenate([kp_ref[:, lanes], knb[:, lanes]], axis=0)
        v_slab = jnp.concatenate([vp_ref[:, lanes], vb[:, lanes]], axis=0)
        k_swap = pltpu.roll(k_slab, HEAD_DIM, axis=1)
        v_swap = pltpu.roll(v_slab, HEAD_DIM, axis=1)
        if g % 2 == 0:
            k_half = (k_slab * keep_lo, k_swap * keep_hi)
            v_half = (v_slab * keep_lo, v_swap * keep_hi)
        else:
            k_half = (k_swap * keep_lo, k_slab * keep_hi)
            v_half = (v_swap * keep_lo, v_slab * keep_hi)
        s0 = 2 * g
        q2 = jnp.concatenate([qn[:, s0 * LANES:(s0 + 1) * LANES], qn[:, (s0 + 1) * LANES:(s0 + 2) * LANES]], axis=0)
        acc = None
        esink = []
        for half in range(2):
            heads = (2 * s0 + half, 2 * s0 + 2 + half)
            t = N_HEADS // N_KV * g + 2 * half
            s = lax.dot_general(q2, k_half[half], (((1,), (1,)), ((), ())), preferred_element_type=F32)
            s = s + bias_ref[first, t:t + 2].reshape(2 * WINDOW, 2 * WINDOW)
            sink = jnp.concatenate([jnp.full((WINDOW, 1), sink_ref[h], F32) for h in heads], axis=0)
            m = jnp.maximum(jnp.max(s, axis=-1, keepdims=True), sink)
            p = jnp.exp(s - m).astype(BF16)
            rhs = jnp.concatenate([v_half[half], ones_half[half]], axis=1)
            oe = jnp.dot(p, rhs, preferred_element_type=F32)
            acc = oe if acc is None else acc + oe
            esink.append(jnp.exp(sink - m))
        denom = acc[:, LANES:] + jnp.where(low_q, esink[0], esink[1])
        out = (acc[:, 0:LANES] / denom).astype(y_ref.dtype)
        y_ref[:, s0 * LANES:(s0 + 1) * LANES] = out[0:WINDOW]
        y_ref[:, (s0 + 1) * LANES:(s0 + 2) * LANES] = out[WINDOW:2 * WINDOW]

    kp_ref[...] = knb
    vp_ref[...] = vb


def _bias_slot_heads():
    per = N_HEADS // N_KV
    return [per * (t // per) + 2 * (t % 2) + (t % per) // 2 for t in range(N_HEADS)]


def attn_prompt(z, bsz, seq, q_norm_g, k_norm_g, sinks, bias):
    assert seq % WINDOW == 0
    nb = seq // WINDOW
    q_col = 6 * GROUP_W // GROUP_W
    kv_col = 7 * GROUP_W // (2 * KV_W)
    qg = jnp.tile(q_norm_g, (1, N_HEADS)) * HEAD_DIM ** -0.5
    kg = jnp.tile(k_norm_g, (1, N_KV))
    e = jnp.asarray(np.kron(np.eye(MXU_COLS // HEAD_DIM), np.ones((HEAD_DIM, HEAD_DIM))), BF16)
    win = pl.BlockSpec((None, WINDOW, KV_W), lambda b, n: (b, 0, 0))
    full = lambda a: pl.BlockSpec(a.shape, lambda b, n: (0,) * a.ndim)
    return pl.pallas_call(
        _attn_prompt_kernel,
        out_shape=(
            jax.ShapeDtypeStruct((bsz * seq, GROUP_W), BF16),
            jax.ShapeDtypeStruct((bsz, WINDOW, KV_W), F32),
            jax.ShapeDtypeStruct((bsz, WINDOW, KV_W), F32),
        ),
        grid=(bsz, nb),
        in_specs=[
            pl.BlockSpec(memory_space=pltpu.SMEM),
            pl.BlockSpec((WINDOW, GROUP_W), lambda b, n: (b * nb + n, q_col)),
            pl.BlockSpec((WINDOW, 2 * KV_W), lambda b, n: (b * nb + n, kv_col)),
            full(qg), full(kg), full(e), full(bias),
        ],
        out_specs=(pl.BlockSpec((WINDOW, GROUP_W), lambda b, n: (b * nb + n, 0)), win, win),
        scratch_shapes=[pltpu.VMEM((WINDOW, KV_W), BF16), pltpu.VMEM((WINDOW, KV_W), BF16)],
        compiler_params=_params("parallel", "arbitrary"),
        name="attn_prompt",
    )(sinks, z, z, qg, kg, e, bias)


def _mix_sample_kernel(sink_ref, z_ref, sa_ref, sp_ref, sc_ref, kb_ref, vb_ref,
                       wa_ref, wp_ref, ps_ref, wc_ref, cb_ref, lg_ref, lb_ref, qg_ref, kg_ref, bias_ref,
                       y_ref, sao_ref, spo_ref, sco_ref, ko_ref, vo_ref, *, start_pos):
    gw = GROUP_W
    zc = lambda c, w=gw: z_ref[:, c:c + w]

    ua = zc(2 * gw) * zc(0)
    conv = wa_ref[0:1, :] * sa_ref[0:1, :] + wa_ref[1:2, :] * sa_ref[1:2, :] + wa_ref[2:3, :] * ua
    y_ref[:, 0:gw] = (zc(gw) * conv).astype(y_ref.dtype)
    sao_ref[0:1, :] = sa_ref[1:2, :]
    sao_ref[1:2, :] = ua

    ub = zc(3 * gw)
    for gi, win in enumerate(POOL_WINDOWS):
        lanes = slice(gi * POOL_GROUP_W, (gi + 1) * POOL_GROUP_W)
        u = ub[:, lanes]
        s = u + jnp.sum(sp_ref[POOL_STATE - (win - 1):POOL_STATE, lanes], axis=0, keepdims=True)
        cnt = float(min(start_pos + 1, win))
        d = s / cnt - u
        yb = jnp.dot(d.astype(BF16), wp_ref[gi], preferred_element_type=F32) * ps_ref[:, lanes]
        y_ref[:, gw + gi * POOL_GROUP_W:gw + (gi + 1) * POOL_GROUP_W] = yb.astype(y_ref.dtype)
    spo_ref[0:POOL_STATE - 1, :] = sp_ref[1:POOL_STATE, :]
    spo_ref[POOL_STATE - 1:POOL_STATE, :] = ub

    uc = zc(4 * gw) * jax.nn.sigmoid(zc(5 * gw))
    kc = CONV_C_W - 1
    acc = jnp.sum(wc_ref[0:kc, :] * sc_ref[...], axis=0, keepdims=True) + wc_ref[kc:kc + 1, :] * uc
    yc = _layer_norm_silu(acc + cb_ref[...], lg_ref[...], lb_ref[...])
    y_ref[:, 2 * gw:3 * gw] = yc.astype(y_ref.dtype)
    sco_ref[0:kc - 1, :] = sc_ref[1:kc, :]
    sco_ref[kc - 1:kc, :] = uc

    wb = kb_ref.shape[0]
    for g in range(N_KV):
        lanes = slice(g * HEAD_DIM, (g + 1) * HEAD_DIM)
        kn = _head_rmsnorm(zc(7 * gw + g * HEAD_DIM, HEAD_DIM), kg_ref[...])
        vn = zc(7 * gw + KV_W + g * HEAD_DIM, HEAD_DIM)
        ko_ref[wb - 1:wb, lanes] = kn
        vo_ref[wb - 1:wb, lanes] = vn
        qs, sinks = [], []
        for j in range(Q_PER_KV):
            h = g * Q_PER_KV + j
            qs.append(_head_rmsnorm(zc(6 * gw + h * HEAD_DIM, HEAD_DIM), qg_ref[...], HEAD_DIM ** -0.5))
            sinks.append(jnp.full((1, 1), sink_ref[h], F32))
        qcat = jnp.concatenate(qs, axis=0)
        sink = jnp.concatenate(sinks, axis=0)
        hb = bias_ref[g * Q_PER_KV:(g + 1) * Q_PER_KV, :]
        s_buf = lax.dot_general(qcat.astype(BF16), kb_ref[:, lanes].astype(BF16), (((1,), (1,)), ((), ())),
                                preferred_element_type=F32) + hb[:, WINDOW - wb:WINDOW]
        s_new = jnp.sum(qcat * kn, axis=-1, keepdims=True) + hb[:, WINDOW:WINDOW + 1]
        m = jnp.maximum(jnp.maximum(jnp.max(s_buf, axis=-1, keepdims=True), s_new), sink)
        p_buf = jnp.exp(s_buf - m)
        p_new = jnp.exp(s_new - m)
        denom = jnp.sum(p_buf, axis=-1, keepdims=True) + p_new + jnp.exp(sink - m)
        o = jnp.dot((p_buf / denom).astype(BF16), vb_ref[:, lanes].astype(BF16), preferred_element_type=F32)
        o = o + (p_new / denom) * vn
        for j in range(Q_PER_KV):
            h = g * Q_PER_KV + j
            y_ref[:, 3 * gw + h * HEAD_DIM:3 * gw + (h + 1) * HEAD_DIM] = o[j:j + 1, :].astype(y_ref.dtype)
    ko_ref[0:wb - 1, :] = kb_ref[1:wb, :]
    vo_ref[0:wb - 1, :] = vb_ref[1:wb, :]


def mix_sample(z, sa, sp, sc, kb, vb, start_pos, conv_a_w, pool_w, pool_scale, conv_c_w, conv_c_b, ln_g, ln_b,
               q_norm_g, k_norm_g, sinks, bias_row):
    nseq, in_cols = z.shape
    wb = kb.shape[1]
    assert wb == WINDOW
    per_seq = lambda a: pl.BlockSpec((None,) + a.shape[1:], lambda s: (s,) + (0,) * (a.ndim - 1))
    full = lambda a: pl.BlockSpec(a.shape, lambda s: (0,) * a.ndim)
    z3 = z.reshape(nseq, 1, in_cols)
    outs = pl.pallas_call(
        functools.partial(_mix_sample_kernel, start_pos=start_pos),
        out_shape=(
            jax.ShapeDtypeStruct((nseq, 1, 4 * GROUP_W), BF16),
            jax.ShapeDtypeStruct(sa.shape, F32), jax.ShapeDtypeStruct(sp.shape, F32),
            jax.ShapeDtypeStruct(sc.shape, F32), jax.ShapeDtypeStruct(kb.shape, F32),
            jax.ShapeDtypeStruct(vb.shape, F32),
        ),
        grid=(nseq,),
        in_specs=[pl.BlockSpec(memory_space=pltpu.SMEM)] + [per_seq(a) for a in (z3, sa, sp, sc, kb, vb)] + [
            full(a) for a in (conv_a_w, pool_w, pool_scale, conv_c_w, conv_c_b, ln_g, ln_b,
                              q_norm_g, k_norm_g, bias_row)],
        out_specs=tuple(per_seq(a) for a in (jax.ShapeDtypeStruct((nseq, 1, 4 * GROUP_W), BF16), sa, sp, sc, kb, vb)),
        compiler_params=_params("parallel"),
        name="mix_sample",
    )(sinks, z3, sa, sp, sc, kb, vb, conv_a_w, pool_w, pool_scale, conv_c_w, conv_c_b, ln_g, ln_b,
      q_norm_g, k_norm_g, bias_row)
    return (outs[0].reshape(nseq, 4 * GROUP_W),) + tuple(outs[1:])


def kernel(x_prompt, x_sample, state_conv_a, state_pool, state_conv_c, cache_k_win, cache_v_win, rel_bias,
           norm_mix_g, w_in, conv_a_w, pool_w, pool_scale, conv_c_w, conv_c_b, ln_c_g, ln_c_b, q_norm_g, k_norm_g,
           attn_sinks, w_out, norm_ffn_g, w_gate, w_up, w_down):
    bsz, seq, d_model = x_prompt.shape
    nseq = x_sample.shape[0]
    depth = w_in.shape[0]
    wb = cache_k_win.shape[2]

    pool_w_b = pool_w.astype(BF16)
    bias = window_bias(rel_bias)
    bias_row = bias[0, np.argsort(_bias_slot_heads()), 0, :]
    kbuf = cache_k_win.reshape(depth, nseq, wb, KV_W)
    vbuf = cache_v_win.reshape(depth, nseq, wb, KV_W)
    row = lambda a, l: a[l][None, :]
    mixer_w = [(conv_a_w[l], pool_w_b[l], row(pool_scale, l), conv_c_w[l], row(conv_c_b, l),
                row(ln_c_g, l), row(ln_c_b, l)) for l in range(depth)]

    xs = x_sample.reshape(nseq, d_model)
    new_s = [[] for _ in range(5)]
    weights_b = []
    for l in range(depth):
        zs, w_in_b = norm_matmul_cast(xs, row(norm_mix_g, l), w_in, l, tn=512)
        y_s, sa_s, sp_s, sc_s, k_s, v_s = mix_sample(
            zs, state_conv_a[l], state_pool[l], state_conv_c[l], kbuf[l], vbuf[l], PAST_LEN,
            *mixer_w[l], row(q_norm_g, l), row(k_norm_g, l), attn_sinks[l], bias_row)
        xs, w_out_b = resid_matmul_cast(y_s, w_out, l, xs, tn=512)
        ffs, w_gate_b, w_up_b = norm_gateup_cast(xs, row(norm_ffn_g, l), w_gate, w_up, l, tn=256)
        xs, w_down_b = resid_matmul_cast(ffs, w_down, l, xs, tn=256)
        weights_b.append((w_in_b, w_out_b, w_gate_b, w_up_b, w_down_b))
        for i, a in enumerate((sa_s, sp_s, sc_s, k_s.reshape(nseq, wb, N_KV, HEAD_DIM),
                               v_s.reshape(nseq, wb, N_KV, HEAD_DIM))):
            new_s[i].append(a)

    xp = x_prompt.reshape(bsz * seq, d_model)
    new_p = [[] for _ in range(5)]
    hb, rs = norm_prep(xp, row(norm_mix_g, 0), tm=512)
    for l in range(depth):
        w_in_b, w_out_b, w_gate_b, w_up_b, w_down_b = weights_b[l]
        zp = scaled_matmul(hb, rs, w_in_b, tm=1024, tn=768)
        y_abc, sa_p, sp_p, sc_p = mix_abc_prompt(zp, bsz, seq, *mixer_w[l], tb=256)
        y_d, k_p, v_p = attn_prompt(zp, bsz, seq, row(q_norm_g, l), row(k_norm_g, l), attn_sinks[l], bias)
        xp, hb, rs = resid_matmul((y_abc, y_d), w_out_b, xp, row(norm_ffn_g, l), tm=1024, tn=512)
        ff = scaled_gateup(hb, rs, w_gate_b, w_up_b, tm=1024, tn=256)
        if l + 1 < depth:
            xp, hb, rs = resid_matmul((ff,), w_down_b, xp, row(norm_mix_g, l + 1), tm=512, tn=512)
        else:
            xp = resid_matmul((ff,), w_down_b, xp, tm=512, tn=512)
        for i, a in enumerate((sa_p, sp_p, sc_p, k_p.reshape(bsz, WINDOW, N_KV, HEAD_DIM),
                               v_p.reshape(bsz, WINDOW, N_KV, HEAD_DIM))):
            new_p[i].append(a)

    return (xp.reshape(bsz, seq, d_model), xs.reshape(nseq, 1, d_model),
            *(jnp.stack(a) for a in new_p), *(jnp.stack(a) for a in new_s))
```

```python
import functools
import math

import numpy as np
import jax
import jax.numpy as jnp
from jax import lax
from jax.experimental import pallas as pl
from jax.experimental.pallas import tpu as pltpu

F32 = jnp.float32
BF16 = jnp.bfloat16

GROUP_W = 1024
HEAD_DIM = 64
N_HEADS = GROUP_W // HEAD_DIM
N_KV = 4
Q_PER_KV = N_HEADS // N_KV
KV_W = N_KV * HEAD_DIM
WINDOW = 128
CONV_A_W = 3
CONV_C_W = 31
POOL_WINDOWS = (2, 4, 8, 16)
POOL_GROUP_W = GROUP_W // len(POOL_WINDOWS)
POOL_STATE = max(POOL_WINDOWS) - 1
NUM_BUCKETS = 32
MAX_DISTANCE = 128
EPS = 1e-6
NEG_INF = float("-inf")

PAST_LEN = 8192

LANES = 128
SUBLANES = 8
SAMPLE_SEQS = 4
MIX_CHUNK = 64
MXU_COLS = 256
HALO = 32
NORM_ROWS = 16
VMEM_LIMIT = 56 << 20


def _params(*sem):
    return pltpu.CompilerParams(dimension_semantics=sem, vmem_limit_bytes=VMEM_LIMIT)


def _rmsnorm_to(x_ref, g_ref, h_ref):
    rows = x_ref.shape[0]
    step = min(NORM_ROWS, rows)

    def body(i, carry):
        r = pl.multiple_of(i * step, step)
        x = x_ref[pl.ds(r, step), :]
        ms = jnp.mean(x * x, axis=-1, keepdims=True)
        h_ref[pl.ds(r, step), :] = (x * lax.rsqrt(ms + EPS) * g_ref[...]).astype(h_ref.dtype)
        return carry

    lax.fori_loop(0, rows // step, body, 0)


def _norm_mm_cast_kernel(x_ref, g_ref, w_ref, o_ref, wb_ref, h_ref):
    @pl.when(pl.program_id(0) == 0)
    def _():
        _rmsnorm_to(x_ref, g_ref, h_ref)

    wb_ref[...] = w_ref[...].astype(BF16)
    o_ref[...] = jnp.dot(h_ref[...], wb_ref[...], preferred_element_type=F32)


def norm_matmul_cast(x, g, w, layer, *, tn):
    m, d = x.shape
    n = w.shape[2]
    assert n % tn == 0
    return pl.pallas_call(
        _norm_mm_cast_kernel,
        out_shape=(jax.ShapeDtypeStruct((m, n), F32), jax.ShapeDtypeStruct((d, n), BF16)),
        grid=(n // tn,),
        in_specs=[
            pl.BlockSpec((m, d), lambda j: (0, 0)),
            pl.BlockSpec((1, d), lambda j: (0, 0)),
            pl.BlockSpec((None, d, tn), lambda j: (layer, 0, j)),
        ],
        out_specs=(pl.BlockSpec((m, tn), lambda j: (0, j)), pl.BlockSpec((d, tn), lambda j: (0, j))),
        scratch_shapes=[pltpu.VMEM((m, d), BF16)],
        compiler_params=_params("arbitrary"),
        name="sample_in_proj",
    )(x, g, w)


def _norm_gateup_cast_kernel(x_ref, g_ref, wg_ref, wu_ref, o_ref, wgb_ref, wub_ref, h_ref):
    @pl.when(pl.program_id(0) == 0)
    def _():
        _rmsnorm_to(x_ref, g_ref, h_ref)

    wgb_ref[...] = wg_ref[...].astype(BF16)
    wub_ref[...] = wu_ref[...].astype(BF16)
    h = h_ref[...]
    a = jnp.dot(h, wgb_ref[...], preferred_element_type=F32)
    b = jnp.dot(h, wub_ref[...], preferred_element_type=F32)
    o_ref[...] = (a * jax.nn.sigmoid(a) * b).astype(o_ref.dtype)


def norm_gateup_cast(x, g, w_gate, w_up, layer, *, tn):
    m, d = x.shape
    f = w_gate.shape[2]
    assert f % tn == 0
    w_in = pl.BlockSpec((None, d, tn), lambda j: (layer, 0, j))
    w_out = pl.BlockSpec((d, tn), lambda j: (0, j))
    return pl.pallas_call(
        _norm_gateup_cast_kernel,
        out_shape=(jax.ShapeDtypeStruct((m, f), BF16), jax.ShapeDtypeStruct((d, f), BF16),
                   jax.ShapeDtypeStruct((d, f), BF16)),
        grid=(f // tn,),
        in_specs=[pl.BlockSpec((m, d), lambda j: (0, 0)), pl.BlockSpec((1, d), lambda j: (0, 0)), w_in, w_in],
        out_specs=(pl.BlockSpec((m, tn), lambda j: (0, j)), w_out, w_out),
        scratch_shapes=[pltpu.VMEM((m, d), BF16)],
        compiler_params=_params("arbitrary"),
        name="sample_gate_up",
    )(x, g, w_gate, w_up)


def _resid_mm_cast_kernel(a_ref, w_ref, r_ref, o_ref, wb_ref):
    wb_ref[...] = w_ref[...].astype(BF16)
    o_ref[...] = r_ref[...] + jnp.dot(a_ref[...], wb_ref[...], preferred_element_type=F32)


def resid_matmul_cast(a, w, layer, resid, *, tn):
    m, n = resid.shape
    k = a.shape[1]
    assert w.shape[1] == k and n % tn == 0
    return pl.pallas_call(
        _resid_mm_cast_kernel,
        out_shape=(jax.ShapeDtypeStruct((m, n), F32), jax.ShapeDtypeStruct((k, n), BF16)),
        grid=(n // tn,),
        in_specs=[
            pl.BlockSpec((m, k), lambda j: (0, 0)),
            pl.BlockSpec((None, k, tn), lambda j: (layer, 0, j)),
            pl.BlockSpec((m, tn), lambda j: (0, j)),
        ],
        out_specs=(pl.BlockSpec((m, tn), lambda j: (0, j)), pl.BlockSpec((k, tn), lambda j: (0, j))),
        compiler_params=_params("arbitrary"),
        name="sample_proj_residual",
    )(a, w, resid)


def _sum_lane_groups(x):
    acc = x[:, 0:LANES]
    for k in range(1, x.shape[1] // LANES):
        acc = acc + x[:, k * LANES:(k + 1) * LANES]
    return acc


def _rowscale_to(ss_ref, rs_ref, d):
    tot = jnp.sum(_sum_lane_groups(ss_ref[...]), axis=-1, keepdims=True)
    rs_ref[...] = jnp.broadcast_to(lax.rsqrt(tot / d + EPS), rs_ref.shape)


def _norm_prep_kernel(x_ref, g_ref, hb_ref, ss_ref):
    rows = x_ref.shape[0]

    def body(i, carry):
        r = pl.multiple_of(i * NORM_ROWS, NORM_ROWS)
        x = x_ref[pl.ds(r, NORM_ROWS), :]
        hb_ref[pl.ds(r, NORM_ROWS), :] = (x * g_ref[...]).astype(BF16)
        ss_ref[pl.ds(r, NORM_ROWS), :] = _sum_lane_groups(x * x)
        return carry

    lax.fori_loop(0, rows // NORM_ROWS, body, 0)


def norm_prep(x, g, *, tm):
    m, d = x.shape
    assert m % tm == 0 and tm % NORM_ROWS == 0
    return pl.pallas_call(
        _norm_prep_kernel,
        out_shape=(jax.ShapeDtypeStruct((m, d), BF16), jax.ShapeDtypeStruct((m, LANES), F32)),
        grid=(m // tm,),
        in_specs=[pl.BlockSpec((tm, d), lambda i: (i, 0)), pl.BlockSpec((1, d), lambda i: (0, 0))],
        out_specs=(pl.BlockSpec((tm, d), lambda i: (i, 0)), pl.BlockSpec((tm, LANES), lambda i: (i, 0))),
        compiler_params=_params("parallel"),
        name="norm_prep",
    )(x, g)


def _scaled_mm_kernel(hb_ref, ss_ref, w_ref, o_ref, rs_ref):
    @pl.when(pl.program_id(1) == 0)
    def _():
        _rowscale_to(ss_ref, rs_ref, hb_ref.shape[1])

    acc = jnp.dot(hb_ref[...], w_ref[...], preferred_element_type=F32)
    rs = rs_ref[...]
    for k in range(o_ref.shape[1] // LANES):
        o_ref[:, k * LANES:(k + 1) * LANES] = acc[:, k * LANES:(k + 1) * LANES] * rs


def scaled_matmul(hb, ss, w, *, tm, tn):
    m, d = hb.shape
    n = w.shape[1]
    assert m % tm == 0 and n % tn == 0 and tn % LANES == 0
    return pl.pallas_call(
        _scaled_mm_kernel,
        out_shape=jax.ShapeDtypeStruct((m, n), F32),
        grid=(m // tm, n // tn),
        in_specs=[
            pl.BlockSpec((tm, d), lambda i, j: (i, 0)),
            pl.BlockSpec((tm, ss.shape[1]), lambda i, j: (i, 0)),
            pl.BlockSpec((d, tn), lambda i, j: (0, j)),
        ],
        out_specs=pl.BlockSpec((tm, tn), lambda i, j: (i, j)),
        scratch_shapes=[pltpu.VMEM((tm, LANES), F32)],
        compiler_params=_params("parallel", "arbitrary"),
        name="in_proj",
    )(hb, ss, w)


def _scaled_gateup_kernel(hb_ref, ss_ref, wg_ref, wu_ref, o_ref, rs_ref):
    @pl.when(pl.program_id(1) == 0)
    def _():
        _rowscale_to(ss_ref, rs_ref, hb_ref.shape[1])

    half = hb_ref.shape[0] // 2
    for r0 in (0, half):
        rows = slice(r0, r0 + half)
        h = hb_ref[rows, :]
        a = jnp.dot(h, wg_ref[...], preferred_element_type=F32)
        b = jnp.dot(h, wu_ref[...], preferred_element_type=F32)
        rs = rs_ref[rows, :]
        for k in range(o_ref.shape[1] // LANES):
            cols = slice(k * LANES, (k + 1) * LANES)
            ak = a[:, cols] * rs
            o_ref[rows, cols] = (ak * jax.nn.sigmoid(ak) * (b[:, cols] * rs)).astype(o_ref.dtype)


def scaled_gateup(hb, ss, w_gate, w_up, *, tm, tn):
    m, d = hb.shape
    f = w_gate.shape[1]
    assert m % tm == 0 and f % tn == 0 and tn % LANES == 0
    w_spec = pl.BlockSpec((d, tn), lambda i, j: (0, j))
    return pl.pallas_call(
        _scaled_gateup_kernel,
        out_shape=jax.ShapeDtypeStruct((m, f), BF16),
        grid=(m // tm, f // tn),
        in_specs=[pl.BlockSpec((tm, d), lambda i, j: (i, 0)), pl.BlockSpec((tm, ss.shape[1]), lambda i, j: (i, 0)),
                  w_spec, w_spec],
        out_specs=pl.BlockSpec((tm, tn), lambda i, j: (i, j)),
        scratch_shapes=[pltpu.VMEM((tm, LANES), F32)],
        compiler_params=_params("parallel", "arbitrary"),
        name="gate_up",
    )(hb, ss, w_gate, w_up)


def _resid_mm_kernel(*refs, k_splits, emit_norm):
    n_a = len(k_splits)
    a_refs = refs[:n_a]
    if emit_norm:
        w_ref, r_ref, g_ref, o_ref, hb_ref, ss_ref = refs[n_a:]
    else:
        w_ref, r_ref, o_ref = refs[n_a:]
    if emit_norm:
        @pl.when(pl.program_id(1) == 0)
        def _():
            ss_ref[...] = jnp.zeros_like(ss_ref)

    tn = o_ref.shape[1]
    part = None
    for c in range(0, tn, MXU_COLS):
        cols = slice(c, c + MXU_COLS)
        acc = r_ref[:, cols]
        off = 0
        for a_ref, k in zip(a_refs, k_splits):
            acc = acc + jnp.dot(a_ref[...], w_ref[off:off + k, cols], preferred_element_type=F32)
            off += k
        o_ref[:, cols] = acc
        if emit_norm:
            hb_ref[:, cols] = (acc * g_ref[:, cols]).astype(BF16)
            ss = _sum_lane_groups(acc * acc)
            part = ss if part is None else part + ss
    if emit_norm:
        ss_ref[...] += part


def resid_matmul(acts, w, resid, g_next=None, *, tm, tn):
    m, n = resid.shape
    k_splits = tuple(a.shape[1] for a in acts)
    k_total = sum(k_splits)
    emit_norm = g_next is not None
    assert w.shape[0] == k_total and m % tm == 0 and n % tn == 0 and tn % MXU_COLS == 0
    tile = pl.BlockSpec((tm, tn), lambda i, j: (i, j))
    in_specs = [pl.BlockSpec((tm, k), lambda i, j: (i, 0)) for k in k_splits] + [
        pl.BlockSpec((k_total, tn), lambda i, j: (0, j)), tile]
    out_shape = jax.ShapeDtypeStruct((m, n), F32)
    out_specs = tile
    args = (*acts, w, resid)
    if emit_norm:
        in_specs.append(pl.BlockSpec((1, tn), lambda i, j: (0, j)))
        out_shape = (out_shape, jax.ShapeDtypeStruct((m, n), BF16), jax.ShapeDtypeStruct((m, LANES), F32))
        out_specs = (tile, tile, pl.BlockSpec((tm, LANES), lambda i, j: (i, 0)))
        args = args + (g_next,)
    return pl.pallas_call(
        functools.partial(_resid_mm_kernel, k_splits=k_splits, emit_norm=emit_norm),
        out_shape=out_shape,
        grid=(m // tm, n // tn),
        in_specs=in_specs,
        out_specs=out_specs,
        compiler_params=_params("parallel", "arbitrary"),
        name="proj_residual",
    )(*args)


def _bucket_matrix():
    i = np.arange(WINDOW)[:, None]
    j = np.arange(2 * WINDOW)[None, :]
    rel = i + WINDOW - j
    max_exact = NUM_BUCKETS // 2
    nf = np.maximum(rel, 1).astype(np.float32)
    large = max_exact + (np.log(nf / max_exact) / math.log(MAX_DISTANCE / max_exact)
                         * (NUM_BUCKETS - max_exact)).astype(np.int32)
    large = np.minimum(large, NUM_BUCKETS - 1)
    bucket = np.where(rel < max_exact, rel, large)
    valid = (rel >= 0) & (rel < WINDOW)
    later = np.where(valid, bucket, -1).astype(np.int32)
    first = np.where(j >= WINDOW, later, -1)
    return np.stack([later, first])


def _bias_kernel(rb_ref, bk_ref, o_ref):
    t = pl.program_id(1)
    per = N_HEADS // N_KV
    h = per * (t // per) + 2 * (t % 2) + (t % per) // 2
    bk = bk_ref[...]
    acc = jnp.full(bk.shape, NEG_INF, F32)
    for b in range(NUM_BUCKETS):
        acc = jnp.where(bk == b, rb_ref[b, h], acc)
    o_ref[...] = acc


def window_bias(rel_bias):
    bucket = jnp.asarray(_bucket_matrix())
    return pl.pallas_call(
        _bias_kernel,
        out_shape=jax.ShapeDtypeStruct((2, N_HEADS, WINDOW, 2 * WINDOW), F32),
        grid=(2, N_HEADS),
        in_specs=[
            pl.BlockSpec(memory_space=pltpu.SMEM),
            pl.BlockSpec((None, WINDOW, 2 * WINDOW), lambda f, h: (f, 0, 0)),
        ],
        out_specs=pl.BlockSpec((None, None, WINDOW, 2 * WINDOW), lambda f, h: (f, h, 0, 0)),
        compiler_params=_params("arbitrary", "arbitrary"),
        name="window_bias",
    )(rel_bias, bucket)


def _layer_norm_silu(y, g, b):
    mu = jnp.mean(y, axis=-1, keepdims=True)
    var = jnp.mean(jnp.square(y - mu), axis=-1, keepdims=True)
    y = (y - mu) * lax.rsqrt(var + EPS) * g + b
    return y * jax.nn.sigmoid(y)


def _delay_rows(blocks, b):
    if b == 0:
        return blocks[1:]
    return [jnp.concatenate([blocks[i], blocks[i + 1]], axis=0)[SUBLANES - b:2 * SUBLANES - b]
            for i in range(len(blocks) - 1)]


def _trailing_sum(blocks, n):
    step = 1
    while step < n:
        blocks = [x + y for x, y in zip(blocks[1:], _delay_rows(blocks, step))]
        step *= 2
    return blocks


def _store_bf16_rows(ref, r0, cols, blocks):
    for p in range(len(blocks) // 2):
        pair = jnp.concatenate([blocks[2 * p], blocks[2 * p + 1]], axis=0)
        ref[pl.ds(r0 + 2 * SUBLANES * p, 2 * SUBLANES), cols] = pair.astype(ref.dtype)


def _mix_abc_kernel(ha_ref, ba_ref, ca_ref, ub_ref, ac_ref, gc_ref,
                    wa_ref, wp_ref, ps_ref, wc_ref, cb_ref, lg_ref, lb_ref,
                    y_ref, sa_ref, sp_ref, sc_ref,
                    ea_ref, eb_ref, ec_ref, d_ref, pre_ref, *, tb):
    t = pl.program_id(1)
    nt = pl.num_programs(1)
    nblk = MIX_CHUNK // SUBLANES
    hist = HALO // SUBLANES

    @pl.when(t == 0)
    def _():
        zeros = jnp.zeros((HALO, GROUP_W), F32)
        ea_ref[0:HALO, :] = zeros
        eb_ref[0:HALO, :] = zeros
        ec_ref[0:HALO, :] = zeros

    ea_ref[HALO:HALO + tb, :] = ca_ref[...] * ha_ref[...]
    eb_ref[HALO:HALO + tb, :] = ub_ref[...]
    ec_ref[HALO:HALO + tb, :] = ac_ref[...] * jax.nn.sigmoid(gc_ref[...])

    def chunk(ci, carry):
        r0 = pl.multiple_of(ci * MIX_CHUNK, MIX_CHUNK)
        load = lambda ref, j, cols: ref[pl.ds(r0 + SUBLANES * j, SUBLANES), cols]
        row_pos = t * tb + r0 + lax.broadcasted_iota(jnp.int32, (SUBLANES, LANES), 0)
        for c in range(GROUP_W // LANES):
            cols = slice(c * LANES, (c + 1) * LANES)

            xa = [load(ea_ref, j, cols) for j in range(hist - 1, hist + nblk)]
            conv = [wa_ref[CONV_A_W - 1:CONV_A_W, cols] * x for x in xa[1:]]
            for b in range(1, CONV_A_W):
                w = wa_ref[CONV_A_W - 1 - b:CONV_A_W - b, cols]
                conv = [x + y for x, y in zip(conv, _delay_rows([w * x for x in xa], b))]
            ya = [ba_ref[pl.ds(r0 + SUBLANES * i, SUBLANES), cols] * conv[i] for i in range(nblk)]
            _store_bf16_rows(y_ref, r0, cols, ya)

            win = POOL_WINDOWS[c * LANES // POOL_GROUP_W]
            xb = [load(eb_ref, j, cols) for j in range(hist + nblk)]
            if win > SUBLANES:
                assert win == 2 * SUBLANES
                sums = _trailing_sum([x + y for x, y in zip(xb[1:], xb[:-1])], SUBLANES)
            else:
                sums = _trailing_sum(xb, win)
            sums = sums[len(sums) - nblk:]
            db = []
            for i in range(nblk):
                cnt = jnp.minimum(row_pos + (SUBLANES * i + 1), win).astype(F32)
                db.append(sums[i] / cnt - xb[hist + i])
            _store_bf16_rows(d_ref, r0, cols, db)

            xc = [load(ec_ref, j, cols) for j in range(hist + nblk)]
            acc = None
            for b in range(SUBLANES):
                q = None
                for a in range(hist):
                    delay = SUBLANES * a + b
                    if delay > CONV_C_W - 1:
                        continue
                    w = wc_ref[CONV_C_W - 1 - delay:CONV_C_W - delay, cols]
                    terms = [w * xc[hist - 1 - a + i] for i in range(nblk + 1)]
                    q = terms if q is None else [x + y for x, y in zip(q, terms)]
                q = _delay_rows(q, b)
                acc = q if acc is None else [x + y for x, y in zip(acc, q)]
            for i in range(nblk):
                pre_ref[pl.ds(r0 + SUBLANES * i, SUBLANES), cols] = acc[i] + cb_ref[:, cols]
        return carry

    lax.fori_loop(0, tb // MIX_CHUNK, chunk, 0)

    for gi in range(len(POOL_WINDOWS)):
        lanes = slice(gi * POOL_GROUP_W, (gi + 1) * POOL_GROUP_W)
        yb = jnp.dot(d_ref[:, lanes], wp_ref[gi], preferred_element_type=F32) * ps_ref[:, lanes]
        y_ref[:, GROUP_W + gi * POOL_GROUP_W:GROUP_W + (gi + 1) * POOL_GROUP_W] = yb.astype(y_ref.dtype)

    yc = _layer_norm_silu(pre_ref[...], lg_ref[...], lb_ref[...])
    y_ref[:, 2 * GROUP_W:3 * GROUP_W] = yc.astype(y_ref.dtype)

    @pl.when(t == nt - 1)
    def _():
        end = HALO + tb
        sa_ref[...] = ea_ref[end - (CONV_A_W - 1):end, :]
        sp_ref[...] = eb_ref[end - POOL_STATE:end, :]
        sc_ref[...] = ec_ref[end - (CONV_C_W - 1):end, :]

    ea_ref[0:HALO, :] = ea_ref[tb:tb + HALO, :]
    eb_ref[0:HALO, :] = eb_ref[tb:tb + HALO, :]
    ec_ref[0:HALO, :] = ec_ref[tb:tb + HALO, :]


def mix_abc_prompt(z, bsz, seq, conv_a_w, pool_w, pool_scale, conv_c_w, conv_c_b, ln_g, ln_b, *, tb):
    assert seq % tb == 0 and tb >= HALO and tb % MIX_CHUNK == 0
    assert HALO % SUBLANES == 0 and HALO >= CONV_C_W - 1 and HALO >= 2 * SUBLANES >= max(POOL_WINDOWS)
    nt = seq // tb
    col = lambda c: pl.BlockSpec((tb, GROUP_W), lambda b, t: (b * nt + t, c))
    full = lambda a: pl.BlockSpec(a.shape, lambda b, t: (0,) * a.ndim)
    state = lambda rows: pl.BlockSpec((None, rows, GROUP_W), lambda b, t: (b, 0, 0))
    ext = pltpu.VMEM((HALO + tb, GROUP_W), F32)
    return pl.pallas_call(
        functools.partial(_mix_abc_kernel, tb=tb),
        out_shape=(
            jax.ShapeDtypeStruct((bsz * seq, 3 * GROUP_W), BF16),
            jax.ShapeDtypeStruct((bsz, CONV_A_W - 1, GROUP_W), F32),
            jax.ShapeDtypeStruct((bsz, POOL_STATE, GROUP_W), F32),
            jax.ShapeDtypeStruct((bsz, CONV_C_W - 1, GROUP_W), F32),
        ),
        grid=(bsz, nt),
        in_specs=[col(c) for c in range(6)] + [
            full(conv_a_w), full(pool_w), full(pool_scale), full(conv_c_w),
            full(conv_c_b), full(ln_g), full(ln_b)],
        out_specs=(
            pl.BlockSpec((tb, 3 * GROUP_W), lambda b, t: (b * nt + t, 0)),
            state(CONV_A_W - 1), state(POOL_STATE), state(CONV_C_W - 1),
        ),
        scratch_shapes=[ext, ext, ext, pltpu.VMEM((tb, GROUP_W), BF16), pltpu.VMEM((tb, GROUP_W), F32)],
        compiler_params=_params("parallel", "arbitrary"),
        name="mix_abc_prompt",
    )(z, z, z, z, z, z, conv_a_w, pool_w, pool_scale, conv_c_w, conv_c_b, ln_g, ln_b)


def _head_rmsnorm(x, g, scale=1.0):
    ms = jnp.mean(x * x, axis=-1, keepdims=True)
    return x * (lax.rsqrt(ms + EPS) * scale) * g


def _headwise_rmsnorm(x, e, g):
    x2 = x * x
    hi = x2.astype(BF16)
    lo = (x2 - hi.astype(F32)).astype(BF16)
    sums = []
    for c in range(0, x.shape[1], MXU_COLS):
        cols = slice(c, c + MXU_COLS)
        sums.append(jnp.dot(hi[:, cols], e, preferred_element_type=F32)
                    + jnp.dot(lo[:, cols], e, preferred_element_type=F32))
    ss = jnp.concatenate(sums, axis=1)
    return x * lax.rsqrt(ss * (1.0 / HEAD_DIM) + EPS) * g


def _attn_prompt_kernel(sink_ref, q_ref, kv_ref, qg_ref, kg_ref, e_ref, bias_ref,
                        y_ref, ko_ref, vo_ref, kp_ref, vp_ref, s_ref, rhs_ref, p_ref, es_ref):
    n = pl.program_id(1)

    @pl.when(n == 0)
    def _():
        kp_ref[...] = jnp.zeros_like(kp_ref)
        vp_ref[...] = jnp.zeros_like(vp_ref)

    first = (n == 0).astype(jnp.int32)
    e = e_ref[...]
    qn = _headwise_rmsnorm(q_ref[...], e, qg_ref[...]).astype(BF16)
    kn = _headwise_rmsnorm(kv_ref[:, 0:KV_W], e, kg_ref[...])
    v = kv_ref[:, KV_W:2 * KV_W]
    ko_ref[...] = kn
    vo_ref[...] = v
    knb = kn.astype(BF16)
    vb = v.astype(BF16)

    low_q = lax.broadcasted_iota(jnp.int32, (2 * WINDOW, LANES), 1) < HEAD_DIM
    ones_half = (jnp.where(low_q, 1.0, 0.0).astype(BF16), jnp.where(low_q, 0.0, 1.0).astype(BF16))
    keep_lo, keep_hi = ones_half
    tiles = [(g, half) for g in range(N_KV) for half in range(2)]

    for i, (g, half) in enumerate(tiles):
        if half == 0:
            lanes = slice((g // 2) * LANES, (g // 2 + 1) * LANES)
            k_slab = jnp.concatenate([kp_ref[:, lanes], knb[:, lanes]], axis=0)
            v_slab = jnp.concatenate([vp_ref[:, lanes], vb[:, lanes]], axis=0)
            k_swap = pltpu.roll(k_slab, HEAD_DIM, axis=1)
            v_swap = pltpu.roll(v_slab, HEAD_DIM, axis=1)
            if g % 2 == 0:
                k_half = (k_slab * keep_lo, k_swap * keep_hi)
                v_half = (v_slab * keep_lo, v_swap * keep_hi)
            else:
                k_half = (k_swap * keep_lo, k_slab * keep_hi)
                v_half = (v_swap * keep_lo, v_slab * keep_hi)
            s0 = 2 * g
            q2 = jnp.concatenate([qn[:, s0 * LANES:(s0 + 1) * LANES], qn[:, (s0 + 1) * LANES:(s0 + 2) * LANES]],
                                 axis=0)
        s_ref[i] = lax.dot_general(q2, k_half[half], (((1,), (1,)), ((), ())), preferred_element_type=F32)
        rhs_ref[i] = jnp.concatenate([v_half[half], ones_half[half]], axis=1)

    for i, (g, half) in enumerate(tiles):
        heads = (4 * g + half, 4 * g + 2 + half)
        t = N_HEADS // N_KV * g + 2 * half
        s = s_ref[i] + bias_ref[first, t:t + 2].reshape(2 * WINDOW, 2 * WINDOW)
        sink = jnp.concatenate([jnp.full((WINDOW, 1), sink_ref[h], F32) for h in heads], axis=0)
        m = jnp.maximum(jnp.max(s, axis=-1, keepdims=True), sink)
        p_ref[i] = jnp.exp(s - m).astype(BF16)
        es_ref[i] = jnp.broadcast_to(jnp.exp(sink - m), (2 * WINDOW, LANES))

    for g in range(N_KV):
        lo, hi = 2 * g, 2 * g + 1
        acc = (jnp.dot(p_ref[lo], rhs_ref[lo], preferred_element_type=F32)
               + jnp.dot(p_ref[hi], rhs_ref[hi], preferred_element_type=F32))
        denom = acc[:, LANES:] + jnp.where(low_q, es_ref[lo], es_ref[hi])
        out = (acc[:, 0:LANES] / denom).astype(y_ref.dtype)
        y_ref[:, lo * LANES:(lo + 1) * LANES] = out[0:WINDOW]
        y_ref[:, hi * LANES:(hi + 1) * LANES] = out[WINDOW:2 * WINDOW]

    kp_ref[...] = knb
    vp_ref[...] = vb


def _bias_slot_heads():
    per = N_HEADS // N_KV
    return [per * (t // per) + 2 * (t % 2) + (t % per) // 2 for t in range(N_HEADS)]


def attn_prompt(z, bsz, seq, q_norm_g, k_norm_g, sinks, bias):
    assert seq % WINDOW == 0
    nb = seq // WINDOW
    q_col = 6 * GROUP_W // GROUP_W
    kv_col = 7 * GROUP_W // (2 * KV_W)
    qg = jnp.tile(q_norm_g, (1, N_HEADS)) * HEAD_DIM ** -0.5
    kg = jnp.tile(k_norm_g, (1, N_KV))
    e = jnp.asarray(np.kron(np.eye(MXU_COLS // HEAD_DIM), np.ones((HEAD_DIM, HEAD_DIM))), BF16)
    win = pl.BlockSpec((None, WINDOW, KV_W), lambda b, n: (b, 0, 0))
    full = lambda a: pl.BlockSpec(a.shape, lambda b, n: (0,) * a.ndim)
    return pl.pallas_call(
        _attn_prompt_kernel,
        out_shape=(
            jax.ShapeDtypeStruct((bsz * seq, GROUP_W), BF16),
            jax.ShapeDtypeStruct((bsz, WINDOW, KV_W), F32),
            jax.ShapeDtypeStruct((bsz, WINDOW, KV_W), F32),
        ),
        grid=(bsz, nb),
        in_specs=[
            pl.BlockSpec(memory_space=pltpu.SMEM),
            pl.BlockSpec((WINDOW, GROUP_W), lambda b, n: (b * nb + n, q_col)),
            pl.BlockSpec((WINDOW, 2 * KV_W), lambda b, n: (b * nb + n, kv_col)),
            full(qg), full(kg), full(e), full(bias),
        ],
        out_specs=(pl.BlockSpec((WINDOW, GROUP_W), lambda b, n: (b * nb + n, 0)), win, win),
        scratch_shapes=[
            pltpu.VMEM((WINDOW, KV_W), BF16), pltpu.VMEM((WINDOW, KV_W), BF16),
            pltpu.VMEM((2 * N_KV, 2 * WINDOW, 2 * WINDOW), F32),
            pltpu.VMEM((2 * N_KV, 2 * WINDOW, 2 * LANES), BF16),
            pltpu.VMEM((2 * N_KV, 2 * WINDOW, 2 * WINDOW), BF16),
            pltpu.VMEM((2 * N_KV, 2 * WINDOW, LANES), F32),
        ],
        compiler_params=_params("parallel", "arbitrary"),
        name="attn_prompt",
    )(sinks, z, z, qg, kg, e, bias)


def _mix_sample_kernel(sink_ref, *refs, start_pos):
    seq_in, shared, seq_out = refs[:6], refs[6:16], refs[16:]
    nseq = seq_in[0].shape[0]
    views = [tuple(r.at[s] for r in seq_in) + tuple(r.at[s] for r in seq_out) for s in range(nseq)]
    for z_ref, sa_ref, sp_ref, sc_ref, _, _, y_ref, sao_ref, spo_ref, sco_ref, _, _ in views:
        _mix_sample_abc(z_ref, sa_ref, sp_ref, sc_ref, *shared[:7], y_ref, sao_ref, spo_ref, sco_ref,
                        start_pos=start_pos)
    _mix_sample_attn(sink_ref, [(v[0], v[4], v[5], v[6], v[10], v[11]) for v in views], *shared[7:])


def _mix_sample_attn(sink_ref, views, qg_ref, kg_ref, bias_ref):
    gw = GROUP_W
    items = [(v, g) for v in views for g in range(N_KV)]
    scored = []
    for (z_ref, kb_ref, vb_ref, y_ref, ko_ref, vo_ref), g in items:
        wb = kb_ref.shape[0]
        zc = lambda c, w: z_ref[:, c:c + w]
        lanes = slice(g * HEAD_DIM, (g + 1) * HEAD_DIM)
        kn = _head_rmsnorm(zc(7 * gw + g * HEAD_DIM, HEAD_DIM), kg_ref[...])
        vn = zc(7 * gw + KV_W + g * HEAD_DIM, HEAD_DIM)
        ko_ref[wb - 1:wb, lanes] = kn
        vo_ref[wb - 1:wb, lanes] = vn
        qs, sinks = [], []
        for j in range(Q_PER_KV):
            h = g * Q_PER_KV + j
            qs.append(_head_rmsnorm(zc(6 * gw + h * HEAD_DIM, HEAD_DIM), qg_ref[...], HEAD_DIM ** -0.5))
            sinks.append(jnp.full((1, 1), sink_ref[h], F32))
        qcat = jnp.concatenate(qs, axis=0)
        sink = jnp.concatenate(sinks, axis=0)
        hb = bias_ref[g * Q_PER_KV:(g + 1) * Q_PER_KV, :]
        s_buf = lax.dot_general(qcat.astype(BF16), kb_ref[:, lanes].astype(BF16), (((1,), (1,)), ((), ())),
                                preferred_element_type=F32) + hb[:, WINDOW - wb:WINDOW]
        s_new = jnp.sum(qcat * kn, axis=-1, keepdims=True) + hb[:, WINDOW:WINDOW + 1]
        scored.append((s_buf, s_new, sink, vn))
    weighted = []
    for s_buf, s_new, sink, vn in scored:
        m = jnp.maximum(jnp.maximum(jnp.max(s_buf, axis=-1, keepdims=True), s_new), sink)
        p_buf = jnp.exp(s_buf - m)
        p_new = jnp.exp(s_new - m)
        denom = jnp.sum(p_buf, axis=-1, keepdims=True) + p_new + jnp.exp(sink - m)
        weighted.append(((p_buf / denom).astype(BF16), (p_new / denom) * vn))
    for ((z_ref, kb_ref, vb_ref, y_ref, ko_ref, vo_ref), g), (p_buf, o_new) in zip(items, weighted):
        lanes = slice(g * HEAD_DIM, (g + 1) * HEAD_DIM)
        o = jnp.dot(p_buf, vb_ref[:, lanes].astype(BF16), preferred_element_type=F32) + o_new
        for j in range(Q_PER_KV):
            h = g * Q_PER_KV + j
            y_ref[:, 3 * gw + h * HEAD_DIM:3 * gw + (h + 1) * HEAD_DIM] = o[j:j + 1, :].astype(y_ref.dtype)
    for _, kb_ref, vb_ref, _, ko_ref, vo_ref in views:
        wb = kb_ref.shape[0]
        ko_ref[0:wb - 1, :] = kb_ref[1:wb, :]
        vo_ref[0:wb - 1, :] = vb_ref[1:wb, :]


def _mix_sample_abc(z_ref, sa_ref, sp_ref, sc_ref, wa_ref, wp_ref, ps_ref, wc_ref, cb_ref, lg_ref, lb_ref,
                    y_ref, sao_ref, spo_ref, sco_ref, *, start_pos):
    gw = GROUP_W
    zc = lambda c, w=gw: z_ref[:, c:c + w]

    ua = zc(2 * gw) * zc(0)
    conv = wa_ref[0:1, :] * sa_ref[0:1, :] + wa_ref[1:2, :] * sa_ref[1:2, :] + wa_ref[2:3, :] * ua
    y_ref[:, 0:gw] = (zc(gw) * conv).astype(y_ref.dtype)
    sao_ref[0:1, :] = sa_ref[1:2, :]
    sao_ref[1:2, :] = ua

    ub = zc(3 * gw)
    for gi, win in enumerate(POOL_WINDOWS):
        lanes = slice(gi * POOL_GROUP_W, (gi + 1) * POOL_GROUP_W)
        u = ub[:, lanes]
        s = u + jnp.sum(sp_ref[POOL_STATE - (win - 1):POOL_STATE, lanes], axis=0, keepdims=True)
        cnt = float(min(start_pos + 1, win))
        d = s / cnt - u
        yb = jnp.dot(d.astype(BF16), wp_ref[gi], preferred_element_type=F32) * ps_ref[:, lanes]
        y_ref[:, gw + gi * POOL_GROUP_W:gw + (gi + 1) * POOL_GROUP_W] = yb.astype(y_ref.dtype)
    spo_ref[0:POOL_STATE - 1, :] = sp_ref[1:POOL_STATE, :]
    spo_ref[POOL_STATE - 1:POOL_STATE, :] = ub

    uc = zc(4 * gw) * jax.nn.sigmoid(zc(5 * gw))
    kc = CONV_C_W - 1
    acc = jnp.sum(wc_ref[0:kc, :] * sc_ref[...], axis=0, keepdims=True) + wc_ref[kc:kc + 1, :] * uc
    yc = _layer_norm_silu(acc + cb_ref[...], lg_ref[...], lb_ref[...])
    y_ref[:, 2 * gw:3 * gw] = yc.astype(y_ref.dtype)
    sco_ref[0:kc - 1, :] = sc_ref[1:kc, :]
    sco_ref[kc - 1:kc, :] = uc


def mix_sample(z, sa, sp, sc, kb, vb, start_pos, conv_a_w, pool_w, pool_scale, conv_c_w, conv_c_b, ln_g, ln_b,
               q_norm_g, k_norm_g, sinks, bias_row):
    nseq, in_cols = z.shape
    wb = kb.shape[1]
    spb = math.gcd(nseq, SAMPLE_SEQS)
    assert wb == WINDOW
    per_seq = lambda a: pl.BlockSpec((spb,) + a.shape[1:], lambda s: (s,) + (0,) * (a.ndim - 1))
    full = lambda a: pl.BlockSpec(a.shape, lambda s: (0,) * a.ndim)
    z3 = z.reshape(nseq, 1, in_cols)
    outs = pl.pallas_call(
        functools.partial(_mix_sample_kernel, start_pos=start_pos),
        out_shape=(
            jax.ShapeDtypeStruct((nseq, 1, 4 * GROUP_W), BF16),
            jax.ShapeDtypeStruct(sa.shape, F32), jax.ShapeDtypeStruct(sp.shape, F32),
            jax.ShapeDtypeStruct(sc.shape, F32), jax.ShapeDtypeStruct(kb.shape, F32),
            jax.ShapeDtypeStruct(vb.shape, F32),
        ),
        grid=(nseq // spb,),
        in_specs=[pl.BlockSpec(memory_space=pltpu.SMEM)] + [per_seq(a) for a in (z3, sa, sp, sc, kb, vb)] + [
            full(a) for a in (conv_a_w, pool_w, pool_scale, conv_c_w, conv_c_b, ln_g, ln_b,
                              q_norm_g, k_norm_g, bias_row)],
        out_specs=tuple(per_seq(a) for a in (jax.ShapeDtypeStruct((nseq, 1, 4 * GROUP_W), BF16), sa, sp, sc, kb, vb)),
        compiler_params=_params("parallel"),
        name="mix_sample",
    )(sinks, z3, sa, sp, sc, kb, vb, conv_a_w, pool_w, pool_scale, conv_c_w, conv_c_b, ln_g, ln_b,
      q_norm_g, k_norm_g, bias_row)
    return (outs[0].reshape(nseq, 4 * GROUP_W),) + tuple(outs[1:])


def kernel(x_prompt, x_sample, state_conv_a, state_pool, state_conv_c, cache_k_win, cache_v_win, rel_bias,
           norm_mix_g, w_in, conv_a_w, pool_w, pool_scale, conv_c_w, conv_c_b, ln_c_g, ln_c_b, q_norm_g, k_norm_g,
           attn_sinks, w_out, norm_ffn_g, w_gate, w_up, w_down):
    bsz, seq, d_model = x_prompt.shape
    nseq = x_sample.shape[0]
    depth = w_in.shape[0]
    wb = cache_k_win.shape[2]

    pool_w_b = pool_w.astype(BF16)
    bias = window_bias(rel_bias)
    bias_row = bias[0, np.argsort(_bias_slot_heads()), 0, :]
    kbuf = cache_k_win.reshape(depth, nseq, wb, KV_W)
    vbuf = cache_v_win.reshape(depth, nseq, wb, KV_W)
    row = lambda a, l: a[l][None, :]
    mixer_w = [(conv_a_w[l], pool_w_b[l], row(pool_scale, l), conv_c_w[l], row(conv_c_b, l),
                row(ln_c_g, l), row(ln_c_b, l)) for l in range(depth)]

    xs = x_sample.reshape(nseq, d_model)
    new_s = [[] for _ in range(5)]
    weights_b = []
    for l in range(depth):
        zs, w_in_b = norm_matmul_cast(xs, row(norm_mix_g, l), w_in, l, tn=512)
        y_s, sa_s, sp_s, sc_s, k_s, v_s = mix_sample(
            zs, state_conv_a[l], state_pool[l], state_conv_c[l], kbuf[l], vbuf[l], PAST_LEN,
            *mixer_w[l], row(q_norm_g, l), row(k_norm_g, l), attn_sinks[l], bias_row)
        xs, w_out_b = resid_matmul_cast(y_s, w_out, l, xs, tn=512)
        ffs, w_gate_b, w_up_b = norm_gateup_cast(xs, row(norm_ffn_g, l), w_gate, w_up, l, tn=256)
        xs, w_down_b = resid_matmul_cast(ffs, w_down, l, xs, tn=256)
        weights_b.append((w_in_b, w_out_b, w_gate_b, w_up_b, w_down_b))
        for i, a in enumerate((sa_s, sp_s, sc_s, k_s.reshape(nseq, wb, N_KV, HEAD_DIM),
                               v_s.reshape(nseq, wb, N_KV, HEAD_DIM))):
            new_s[i].append(a)

    xp = x_prompt.reshape(bsz * seq, d_model)
    new_p = [[] for _ in range(5)]
    hb, rs = norm_prep(xp, row(norm_mix_g, 0), tm=512)
    for l in range(depth):
        w_in_b, w_out_b, w_gate_b, w_up_b, w_down_b = weights_b[l]
        zp = scaled_matmul(hb, rs, w_in_b, tm=1024, tn=768)
        y_abc, sa_p, sp_p, sc_p = mix_abc_prompt(zp, bsz, seq, *mixer_w[l], tb=256)
        y_d, k_p, v_p = attn_prompt(zp, bsz, seq, row(q_norm_g, l), row(k_norm_g, l), attn_sinks[l], bias)
        xp, hb, rs = resid_matmul((y_abc, y_d), w_out_b, xp, row(norm_ffn_g, l), tm=1024, tn=512)
        ff = scaled_gateup(hb, rs, w_gate_b, w_up_b, tm=2048, tn=256)
        if l + 1 < depth:
            xp, hb, rs = resid_matmul((ff,), w_down_b, xp, row(norm_mix_g, l + 1), tm=512, tn=512)
        else:
            xp = resid_matmul((ff,), w_down_b, xp, tm=512, tn=512)
        for i, a in enumerate((sa_p, sp_p, sc_p, k_p.reshape(bsz, WINDOW, N_KV, HEAD_DIM),
                               v_p.reshape(bsz, WINDOW, N_KV, HEAD_DIM))):
            new_p[i].append(a)

    return (xp.reshape(bsz, seq, d_model), xs.reshape(nseq, 1, d_model),
            *(jnp.stack(a) for a in new_p), *(jnp.stack(a) for a in new_s))
```

```python
import functools
import math

import numpy as np
import jax
import jax.numpy as jnp
from jax import lax
from jax.experimental import pallas as pl
from jax.experimental.pallas import tpu as pltpu

F32 = jnp.float32
BF16 = jnp.bfloat16

GROUP_W = 1024
HEAD_DIM = 64
N_HEADS = GROUP_W // HEAD_DIM
N_KV = 4
Q_PER_KV = N_HEADS // N_KV
KV_W = N_KV * HEAD_DIM
WINDOW = 128
CONV_A_W = 3
CONV_C_W = 31
POOL_WINDOWS = (2, 4, 8, 16)
POOL_GROUP_W = GROUP_W // len(POOL_WINDOWS)
POOL_STATE = max(POOL_WINDOWS) - 1
NUM_BUCKETS = 32
MAX_DISTANCE = 128
EPS = 1e-6
NEG_INF = float("-inf")

PAST_LEN = 8192

LANES = 128
SUBLANES = 8
SAMPLE_SEQS = 4
MIX_CHUNK = 64
MXU_COLS = 256
HALO = 32
NORM_ROWS = 16
VMEM_LIMIT = 56 << 20


def _params(*sem):
    return pltpu.CompilerParams(dimension_semantics=sem, vmem_limit_bytes=VMEM_LIMIT)


def _rmsnorm_to(x_ref, g_ref, h_ref):
    rows = x_ref.shape[0]
    step = min(NORM_ROWS, rows)

    def body(i, carry):
        r = pl.multiple_of(i * step, step)
        x = x_ref[pl.ds(r, step), :]
        ms = jnp.mean(x * x, axis=-1, keepdims=True)
        h_ref[pl.ds(r, step), :] = (x * lax.rsqrt(ms + EPS) * g_ref[...]).astype(h_ref.dtype)
        return carry

    lax.fori_loop(0, rows // step, body, 0)


def _sample_w_spec(w, layer, tn):
    if layer is None:
        return pl.BlockSpec((w.shape[0], tn), lambda j: (0, j))
    return pl.BlockSpec((None, w.shape[1], tn), lambda j: (layer, 0, j))


def _bf16_tile(w_ref, wb_ref):
    if wb_ref is None:
        return w_ref[...]
    wb_ref[...] = w_ref[...].astype(BF16)
    return wb_ref[...]


def _norm_mm_small_kernel(x_ref, g_ref, w_ref, o_ref, *rest):
    wb_ref, h_ref = rest if len(rest) == 2 else (None, rest[0])

    @pl.when(pl.program_id(0) == 0)
    def _():
        _rmsnorm_to(x_ref, g_ref, h_ref)

    o_ref[...] = jnp.dot(h_ref[...], _bf16_tile(w_ref, wb_ref), preferred_element_type=F32)


def norm_matmul_small(x, g, w, layer, *, tn):
    m, d = x.shape
    n = w.shape[-1]
    assert n % tn == 0
    out_shape = [jax.ShapeDtypeStruct((m, n), F32)]
    out_specs = [pl.BlockSpec((m, tn), lambda j: (0, j))]
    if layer is not None:
        out_shape.append(jax.ShapeDtypeStruct((d, n), BF16))
        out_specs.append(pl.BlockSpec((d, tn), lambda j: (0, j)))
    return pl.pallas_call(
        _norm_mm_small_kernel,
        out_shape=tuple(out_shape),
        grid=(n // tn,),
        in_specs=[pl.BlockSpec((m, d), lambda j: (0, 0)), pl.BlockSpec((1, d), lambda j: (0, 0)),
                  _sample_w_spec(w, layer, tn)],
        out_specs=tuple(out_specs),
        scratch_shapes=[pltpu.VMEM((m, d), BF16)],
        compiler_params=_params("arbitrary"),
        name="sample_in_proj",
    )(x, g, w)


def _norm_gateup_small_kernel(x_ref, g_ref, wg_ref, wu_ref, o_ref, *rest):
    wgb_ref, wub_ref, h_ref = rest if len(rest) == 3 else (None, None, rest[0])

    @pl.when(pl.program_id(0) == 0)
    def _():
        _rmsnorm_to(x_ref, g_ref, h_ref)

    h = h_ref[...]
    a = jnp.dot(h, _bf16_tile(wg_ref, wgb_ref), preferred_element_type=F32)
    b = jnp.dot(h, _bf16_tile(wu_ref, wub_ref), preferred_element_type=F32)
    o_ref[...] = (a * jax.nn.sigmoid(a) * b).astype(o_ref.dtype)


def norm_gateup_small(x, g, w_gate, w_up, layer, *, tn):
    m, d = x.shape
    f = w_gate.shape[-1]
    assert f % tn == 0
    out_shape = [jax.ShapeDtypeStruct((m, f), BF16)]
    out_specs = [pl.BlockSpec((m, tn), lambda j: (0, j))]
    if layer is not None:
        out_shape += [jax.ShapeDtypeStruct((d, f), BF16)] * 2
        out_specs += [pl.BlockSpec((d, tn), lambda j: (0, j))] * 2
    return pl.pallas_call(
        _norm_gateup_small_kernel,
        out_shape=tuple(out_shape),
        grid=(f // tn,),
        in_specs=[pl.BlockSpec((m, d), lambda j: (0, 0)), pl.BlockSpec((1, d), lambda j: (0, 0)),
                  _sample_w_spec(w_gate, layer, tn), _sample_w_spec(w_up, layer, tn)],
        out_specs=tuple(out_specs),
        scratch_shapes=[pltpu.VMEM((m, d), BF16)],
        compiler_params=_params("arbitrary"),
        name="sample_gate_up",
    )(x, g, w_gate, w_up)


def _resid_mm_small_kernel(a_ref, w_ref, r_ref, o_ref, wb_ref=None):
    o_ref[...] = r_ref[...] + jnp.dot(a_ref[...], _bf16_tile(w_ref, wb_ref), preferred_element_type=F32)


def resid_matmul_small(a, w, layer, resid, *, tn):
    m, n = resid.shape
    k = a.shape[1]
    assert w.shape[-2] == k and n % tn == 0
    out_shape = [jax.ShapeDtypeStruct((m, n), F32)]
    out_specs = [pl.BlockSpec((m, tn), lambda j: (0, j))]
    if layer is not None:
        out_shape.append(jax.ShapeDtypeStruct((k, n), BF16))
        out_specs.append(pl.BlockSpec((k, tn), lambda j: (0, j)))
    return pl.pallas_call(
        _resid_mm_small_kernel,
        out_shape=tuple(out_shape),
        grid=(n // tn,),
        in_specs=[pl.BlockSpec((m, k), lambda j: (0, 0)), _sample_w_spec(w, layer, tn),
                  pl.BlockSpec((m, tn), lambda j: (0, j))],
        out_specs=tuple(out_specs),
        compiler_params=_params("arbitrary"),
        name="sample_proj_residual",
    )(a, w, resid)


def _sum_lane_groups(x):
    acc = x[:, 0:LANES]
    for k in range(1, x.shape[1] // LANES):
        acc = acc + x[:, k * LANES:(k + 1) * LANES]
    return acc


def _rowscale_to(ss_ref, rs_ref, d):
    tot = jnp.sum(_sum_lane_groups(ss_ref[...]), axis=-1, keepdims=True)
    rs_ref[...] = jnp.broadcast_to(lax.rsqrt(tot / d + EPS), rs_ref.shape)


def _norm_prep_kernel(x_ref, g_ref, hb_ref, ss_ref):
    rows = x_ref.shape[0]

    def body(i, carry):
        r = pl.multiple_of(i * NORM_ROWS, NORM_ROWS)
        x = x_ref[pl.ds(r, NORM_ROWS), :]
        hb_ref[pl.ds(r, NORM_ROWS), :] = (x * g_ref[...]).astype(BF16)
        ss_ref[pl.ds(r, NORM_ROWS), :] = _sum_lane_groups(x * x)
        return carry

    lax.fori_loop(0, rows // NORM_ROWS, body, 0)


def norm_prep(x, g, *, tm):
    m, d = x.shape
    assert m % tm == 0 and tm % NORM_ROWS == 0
    return pl.pallas_call(
        _norm_prep_kernel,
        out_shape=(jax.ShapeDtypeStruct((m, d), BF16), jax.ShapeDtypeStruct((m, LANES), F32)),
        grid=(m // tm,),
        in_specs=[pl.BlockSpec((tm, d), lambda i: (i, 0)), pl.BlockSpec((1, d), lambda i: (0, 0))],
        out_specs=(pl.BlockSpec((tm, d), lambda i: (i, 0)), pl.BlockSpec((tm, LANES), lambda i: (i, 0))),
        compiler_params=_params("parallel"),
        name="norm_prep",
    )(x, g)


def _side_cast_specs(side_casts, gm, gn):
    in_specs, out_specs, out_shape, args = [], [], [], []
    for w, layer, swap in side_casts:
        _, r, c = w.shape
        gr, gc = (gn, gm) if swap else (gm, gn)
        assert r % (gr * 2 * SUBLANES) == 0 and c % (gc * LANES) == 0
        block = (r // gr, c // gc)
        pick = (lambda i, j: (j, i)) if swap else (lambda i, j: (i, j))
        in_specs.append(pl.BlockSpec((None,) + block, lambda i, j, pick=pick, layer=layer: (layer,) + pick(i, j)))
        out_specs.append(pl.BlockSpec(block, pick))
        out_shape.append(jax.ShapeDtypeStruct((r, c), BF16))
        args.append(w)
    return in_specs, out_specs, out_shape, args


def _run_side_casts(src_refs, dst_refs):
    for src, dst in zip(src_refs, dst_refs):
        dst[...] = src[...].astype(BF16)


def _scaled_mm_kernel(hb_ref, ss_ref, w_ref, *refs):
    n_side = (len(refs) - 2) // 2
    o_ref, rs_ref = refs[n_side], refs[-1]

    @pl.when(pl.program_id(1) == 0)
    def _():
        _rowscale_to(ss_ref, rs_ref, hb_ref.shape[1])

    _run_side_casts(refs[:n_side], refs[n_side + 1:-1])
    acc = jnp.dot(hb_ref[...], w_ref[...], preferred_element_type=F32)
    rs = rs_ref[...]
    for k in range(o_ref.shape[1] // LANES):
        o_ref[:, k * LANES:(k + 1) * LANES] = acc[:, k * LANES:(k + 1) * LANES] * rs


def scaled_matmul(hb, ss, w, side_casts=(), *, tm, tn):
    m, d = hb.shape
    n = w.shape[1]
    assert m % tm == 0 and n % tn == 0 and tn % LANES == 0
    s_in, s_out, s_shape, s_args = _side_cast_specs(side_casts, m // tm, n // tn)
    return pl.pallas_call(
        _scaled_mm_kernel,
        out_shape=(jax.ShapeDtypeStruct((m, n), F32), *s_shape),
        grid=(m // tm, n // tn),
        in_specs=[
            pl.BlockSpec((tm, d), lambda i, j: (i, 0)),
            pl.BlockSpec((tm, ss.shape[1]), lambda i, j: (i, 0)),
            pl.BlockSpec((d, tn), lambda i, j: (0, j)),
            *s_in,
        ],
        out_specs=(pl.BlockSpec((tm, tn), lambda i, j: (i, j)), *s_out),
        scratch_shapes=[pltpu.VMEM((tm, LANES), F32)],
        compiler_params=_params("parallel", "arbitrary"),
        name="in_proj",
    )(hb, ss, w, *s_args)


def _scaled_gateup_kernel(hb_ref, ss_ref, wg_ref, wu_ref, *refs):
    n_side = (len(refs) - 2) // 2
    o_ref, rs_ref = refs[n_side], refs[-1]

    @pl.when(pl.program_id(1) == 0)
    def _():
        _rowscale_to(ss_ref, rs_ref, hb_ref.shape[1])

    _run_side_casts(refs[:n_side], refs[n_side + 1:-1])
    half = hb_ref.shape[0] // 2
    for r0 in (0, half):
        rows = slice(r0, r0 + half)
        h = hb_ref[rows, :]
        a = jnp.dot(h, wg_ref[...], preferred_element_type=F32)
        b = jnp.dot(h, wu_ref[...], preferred_element_type=F32)
        rs = rs_ref[rows, :]
        for k in range(o_ref.shape[1] // LANES):
            cols = slice(k * LANES, (k + 1) * LANES)
            ak = a[:, cols] * rs
            o_ref[rows, cols] = (ak * jax.nn.sigmoid(ak) * (b[:, cols] * rs)).astype(o_ref.dtype)


def scaled_gateup(hb, ss, w_gate, w_up, side_casts=(), *, tm, tn):
    m, d = hb.shape
    f = w_gate.shape[1]
    assert m % tm == 0 and f % tn == 0 and tn % LANES == 0
    w_spec = pl.BlockSpec((d, tn), lambda i, j: (0, j))
    s_in, s_out, s_shape, s_args = _side_cast_specs(side_casts, m // tm, f // tn)
    return pl.pallas_call(
        _scaled_gateup_kernel,
        out_shape=(jax.ShapeDtypeStruct((m, f), BF16), *s_shape),
        grid=(m // tm, f // tn),
        in_specs=[
            pl.BlockSpec((tm, d), lambda i, j: (i, 0), pipeline_mode=pl.Buffered(1)),
            pl.BlockSpec((tm, ss.shape[1]), lambda i, j: (i, 0)),
            w_spec, w_spec, *s_in],
        out_specs=(pl.BlockSpec((tm, tn), lambda i, j: (i, j)), *s_out),
        scratch_shapes=[pltpu.VMEM((tm, LANES), F32)],
        compiler_params=_params("parallel", "arbitrary"),
        name="gate_up",
    )(hb, ss, w_gate, w_up, *s_args)


def _resid_mm_kernel(*refs, k_splits, emit_norm, n_side):
    n_a = len(k_splits)
    a_refs = refs[:n_a]
    n_in = 3 if emit_norm else 2
    n_out = 3 if emit_norm else 1
    main_in = refs[n_a:n_a + n_in]
    side_in = refs[n_a + n_in:n_a + n_in + n_side]
    main_out = refs[n_a + n_in + n_side:n_a + n_in + n_side + n_out]
    side_out = refs[n_a + n_in + n_side + n_out:]
    if emit_norm:
        w_ref, r_ref, g_ref = main_in
        o_ref, hb_ref, ss_ref = main_out
    else:
        w_ref, r_ref = main_in
        o_ref, = main_out
    _run_side_casts(side_in, side_out)
    if emit_norm:
        @pl.when(pl.program_id(1) == 0)
        def _():
            ss_ref[...] = jnp.zeros_like(ss_ref)

    tn = o_ref.shape[1]
    part = None
    for c in range(0, tn, MXU_COLS):
        cols = slice(c, c + MXU_COLS)
        acc = r_ref[:, cols]
        off = 0
        for a_ref, k in zip(a_refs, k_splits):
            acc = acc + jnp.dot(a_ref[...], w_ref[off:off + k, cols], preferred_element_type=F32)
            off += k
        o_ref[:, cols] = acc
        if emit_norm:
            hb_ref[:, cols] = (acc * g_ref[:, cols]).astype(BF16)
            ss = _sum_lane_groups(acc * acc)
            part = ss if part is None else part + ss
    if emit_norm:
        ss_ref[...] += part


def resid_matmul(acts, w, resid, g_next=None, side_casts=(), *, tm, tn):
    m, n = resid.shape
    k_splits = tuple(a.shape[1] for a in acts)
    k_total = sum(k_splits)
    emit_norm = g_next is not None
    assert w.shape[0] == k_total and m % tm == 0 and n % tn == 0 and tn % MXU_COLS == 0
    tile = pl.BlockSpec((tm, tn), lambda i, j: (i, j))
    in_specs = [pl.BlockSpec((tm, k), lambda i, j: (i, 0)) for k in k_splits] + [
        pl.BlockSpec((k_total, tn), lambda i, j: (0, j)), tile]
    out_shape = [jax.ShapeDtypeStruct((m, n), F32)]
    out_specs = [tile]
    args = (*acts, w, resid)
    if emit_norm:
        in_specs.append(pl.BlockSpec((1, tn), lambda i, j: (0, j)))
        out_shape += [jax.ShapeDtypeStruct((m, n), BF16), jax.ShapeDtypeStruct((m, LANES), F32)]
        out_specs += [tile, pl.BlockSpec((tm, LANES), lambda i, j: (i, 0))]
        args = args + (g_next,)
    s_in, s_out, s_shape, s_args = _side_cast_specs(side_casts, m // tm, n // tn)
    return pl.pallas_call(
        functools.partial(_resid_mm_kernel, k_splits=k_splits, emit_norm=emit_norm, n_side=len(s_args)),
        out_shape=(*out_shape, *s_shape),
        grid=(m // tm, n // tn),
        in_specs=in_specs + s_in,
        out_specs=(*out_specs, *s_out),
        compiler_params=_params("parallel", "arbitrary"),
        name="proj_residual",
    )(*args, *s_args)


def _bucket_matrix():
    i = np.arange(WINDOW)[:, None]
    j = np.arange(2 * WINDOW)[None, :]
    rel = i + WINDOW - j
    max_exact = NUM_BUCKETS // 2
    nf = np.maximum(rel, 1).astype(np.float32)
    large = max_exact + (np.log(nf / max_exact) / math.log(MAX_DISTANCE / max_exact)
                         * (NUM_BUCKETS - max_exact)).astype(np.int32)
    large = np.minimum(large, NUM_BUCKETS - 1)
    bucket = np.where(rel < max_exact, rel, large)
    valid = (rel >= 0) & (rel < WINDOW)
    later = np.where(valid, bucket, -1).astype(np.int32)
    first = np.where(j >= WINDOW, later, -1)
    return np.stack([later, first])


def _bias_kernel(rb_ref, bk_ref, o_ref):
    t = pl.program_id(1)
    per = N_HEADS // N_KV
    h = per * (t // per) + 2 * (t % 2) + (t % per) // 2
    bk = bk_ref[...]
    acc = jnp.full(bk.shape, NEG_INF, F32)
    for b in range(NUM_BUCKETS):
        acc = jnp.where(bk == b, rb_ref[b, h], acc)
    o_ref[...] = acc


def window_bias(rel_bias):
    bucket = jnp.asarray(_bucket_matrix())
    return pl.pallas_call(
        _bias_kernel,
        out_shape=jax.ShapeDtypeStruct((2, N_HEADS, WINDOW, 2 * WINDOW), F32),
        grid=(2, N_HEADS),
        in_specs=[
            pl.BlockSpec(memory_space=pltpu.SMEM),
            pl.BlockSpec((None, WINDOW, 2 * WINDOW), lambda f, h: (f, 0, 0)),
        ],
        out_specs=pl.BlockSpec((None, None, WINDOW, 2 * WINDOW), lambda f, h: (f, h, 0, 0)),
        compiler_params=_params("arbitrary", "arbitrary"),
        name="window_bias",
    )(rel_bias, bucket)


def _layer_norm_silu(y, g, b):
    mu = jnp.mean(y, axis=-1, keepdims=True)
    var = jnp.mean(jnp.square(y - mu), axis=-1, keepdims=True)
    y = (y - mu) * lax.rsqrt(var + EPS) * g + b
    return y * jax.nn.sigmoid(y)


def _delay_rows(blocks, b):
    if b == 0:
        return blocks[1:]
    return [jnp.concatenate([blocks[i], blocks[i + 1]], axis=0)[SUBLANES - b:2 * SUBLANES - b]
            for i in range(len(blocks) - 1)]


def _trailing_sum(blocks, n):
    step = 1
    while step < n:
        blocks = [x + y for x, y in zip(blocks[1:], _delay_rows(blocks, step))]
        step *= 2
    return blocks


def _store_bf16_rows(ref, r0, cols, blocks):
    for p in range(len(blocks) // 2):
        pair = jnp.concatenate([blocks[2 * p], blocks[2 * p + 1]], axis=0)
        ref[pl.ds(r0 + 2 * SUBLANES * p, 2 * SUBLANES), cols] = pair.astype(ref.dtype)


def _mix_abc_kernel(ha_ref, ba_ref, ca_ref, ub_ref, ac_ref, gc_ref,
                    wa_ref, wp_ref, ps_ref, wc_ref, cb_ref, lg_ref, lb_ref,
                    y_ref, sa_ref, sp_ref, sc_ref,
                    ea_ref, eb_ref, ec_ref, d_ref, pre_ref, *, tb):
    t = pl.program_id(1)
    nt = pl.num_programs(1)
    nblk = MIX_CHUNK // SUBLANES
    hist = HALO // SUBLANES

    @pl.when(t == 0)
    def _():
        zeros = jnp.zeros((HALO, GROUP_W), F32)
        ea_ref[0:HALO, :] = zeros
        eb_ref[0:HALO, :] = zeros
        ec_ref[0:HALO, :] = zeros

    ea_ref[HALO:HALO + tb, :] = ca_ref[...] * ha_ref[...]
    eb_ref[HALO:HALO + tb, :] = ub_ref[...]
    ec_ref[HALO:HALO + tb, :] = ac_ref[...] * jax.nn.sigmoid(gc_ref[...])

    def chunk(ci, carry):
        r0 = pl.multiple_of(ci * MIX_CHUNK, MIX_CHUNK)
        load = lambda ref, j, cols: ref[pl.ds(r0 + SUBLANES * j, SUBLANES), cols]
        row_pos = t * tb + r0 + lax.broadcasted_iota(jnp.int32, (SUBLANES, LANES), 0)
        for c in range(GROUP_W // LANES):
            cols = slice(c * LANES, (c + 1) * LANES)

            xa = [load(ea_ref, j, cols) for j in range(hist - 1, hist + nblk)]
            conv = [wa_ref[CONV_A_W - 1:CONV_A_W, cols] * x for x in xa[1:]]
            for b in range(1, CONV_A_W):
                w = wa_ref[CONV_A_W - 1 - b:CONV_A_W - b, cols]
                conv = [x + y for x, y in zip(conv, _delay_rows([w * x for x in xa], b))]
            ya = [ba_ref[pl.ds(r0 + SUBLANES * i, SUBLANES), cols] * conv[i] for i in range(nblk)]
            _store_bf16_rows(y_ref, r0, cols, ya)

            win = POOL_WINDOWS[c * LANES // POOL_GROUP_W]
            xb = [load(eb_ref, j, cols) for j in range(hist + nblk)]
            if win > SUBLANES:
                assert win == 2 * SUBLANES
                sums = _trailing_sum([x + y for x, y in zip(xb[1:], xb[:-1])], SUBLANES)
            else:
                sums = _trailing_sum(xb, win)
            sums = sums[len(sums) - nblk:]
            db = []
            for i in range(nblk):
                cnt = jnp.minimum(row_pos + (SUBLANES * i + 1), win).astype(F32)
                db.append(sums[i] / cnt - xb[hist + i])
            _store_bf16_rows(d_ref, r0, cols, db)

            xc = [load(ec_ref, j, cols) for j in range(hist + nblk)]
            acc = None
            for b in range(SUBLANES):
                q = None
                for a in range(hist):
                    delay = SUBLANES * a + b
                    if delay > CONV_C_W - 1:
                        continue
                    w = wc_ref[CONV_C_W - 1 - delay:CONV_C_W - delay, cols]
                    terms = [w * xc[hist - 1 - a + i] for i in range(nblk + 1)]
                    q = terms if q is None else [x + y for x, y in zip(q, terms)]
                q = _delay_rows(q, b)
                acc = q if acc is None else [x + y for x, y in zip(acc, q)]
            for i in range(nblk):
                pre_ref[pl.ds(r0 + SUBLANES * i, SUBLANES), cols] = acc[i] + cb_ref[:, cols]
        return carry

    lax.fori_loop(0, tb // MIX_CHUNK, chunk, 0)

    for gi in range(len(POOL_WINDOWS)):
        lanes = slice(gi * POOL_GROUP_W, (gi + 1) * POOL_GROUP_W)
        yb = jnp.dot(d_ref[:, lanes], wp_ref[gi], preferred_element_type=F32) * ps_ref[:, lanes]
        y_ref[:, GROUP_W + gi * POOL_GROUP_W:GROUP_W + (gi + 1) * POOL_GROUP_W] = yb.astype(y_ref.dtype)

    yc = _layer_norm_silu(pre_ref[...], lg_ref[...], lb_ref[...])
    y_ref[:, 2 * GROUP_W:3 * GROUP_W] = yc.astype(y_ref.dtype)

    @pl.when(t == nt - 1)
    def _():
        end = HALO + tb
        sa_ref[...] = ea_ref[end - (CONV_A_W - 1):end, :]
        sp_ref[...] = eb_ref[end - POOL_STATE:end, :]
        sc_ref[...] = ec_ref[end - (CONV_C_W - 1):end, :]

    ea_ref[0:HALO, :] = ea_ref[tb:tb + HALO, :]
    eb_ref[0:HALO, :] = eb_ref[tb:tb + HALO, :]
    ec_ref[0:HALO, :] = ec_ref[tb:tb + HALO, :]


def mix_abc_prompt(z, bsz, seq, conv_a_w, pool_w, pool_scale, conv_c_w, conv_c_b, ln_g, ln_b, *, tb):
    assert seq % tb == 0 and tb >= HALO and tb % MIX_CHUNK == 0
    assert HALO % SUBLANES == 0 and HALO >= CONV_C_W - 1 and HALO >= 2 * SUBLANES >= max(POOL_WINDOWS)
    nt = seq // tb
    col = lambda c: pl.BlockSpec((tb, GROUP_W), lambda b, t: (b * nt + t, c))
    full = lambda a: pl.BlockSpec(a.shape, lambda b, t: (0,) * a.ndim)
    state = lambda rows: pl.BlockSpec((None, rows, GROUP_W), lambda b, t: (b, 0, 0))
    ext = pltpu.VMEM((HALO + tb, GROUP_W), F32)
    return pl.pallas_call(
        functools.partial(_mix_abc_kernel, tb=tb),
        out_shape=(
            jax.ShapeDtypeStruct((bsz * seq, 3 * GROUP_W), BF16),
            jax.ShapeDtypeStruct((bsz, CONV_A_W - 1, GROUP_W), F32),
            jax.ShapeDtypeStruct((bsz, POOL_STATE, GROUP_W), F32),
            jax.ShapeDtypeStruct((bsz, CONV_C_W - 1, GROUP_W), F32),
        ),
        grid=(bsz, nt),
        in_specs=[col(c) for c in range(6)] + [
            full(conv_a_w), full(pool_w), full(pool_scale), full(conv_c_w),
            full(conv_c_b), full(ln_g), full(ln_b)],
        out_specs=(
            pl.BlockSpec((tb, 3 * GROUP_W), lambda b, t: (b * nt + t, 0)),
            state(CONV_A_W - 1), state(POOL_STATE), state(CONV_C_W - 1),
        ),
        scratch_shapes=[ext, ext, ext, pltpu.VMEM((tb, GROUP_W), BF16), pltpu.VMEM((tb, GROUP_W), F32)],
        compiler_params=_params("parallel", "arbitrary"),
        name="mix_abc_prompt",
    )(z, z, z, z, z, z, conv_a_w, pool_w, pool_scale, conv_c_w, conv_c_b, ln_g, ln_b)


def _head_rmsnorm(x, g, scale=1.0):
    ms = jnp.mean(x * x, axis=-1, keepdims=True)
    return x * (lax.rsqrt(ms + EPS) * scale) * g


def _headwise_rmsnorm(x, e, g):
    x2 = x * x
    hi = x2.astype(BF16)
    lo = (x2 - hi.astype(F32)).astype(BF16)
    sums = []
    for c in range(0, x.shape[1], MXU_COLS):
        cols = slice(c, c + MXU_COLS)
        sums.append(jnp.dot(hi[:, cols], e, preferred_element_type=F32)
                    + jnp.dot(lo[:, cols], e, preferred_element_type=F32))
    ss = jnp.concatenate(sums, axis=1)
    return x * lax.rsqrt(ss * (1.0 / HEAD_DIM) + EPS) * g


def _attn_prompt_kernel(sink_ref, q_ref, kv_ref, qg_ref, kg_ref, e_ref, bias_ref,
                        y_ref, ko_ref, vo_ref, kp_ref, vp_ref, s_ref, rhs_ref, p_ref, es_ref):
    n = pl.program_id(1)

    @pl.when(n == 0)
    def _():
        kp_ref[...] = jnp.zeros_like(kp_ref)
        vp_ref[...] = jnp.zeros_like(vp_ref)

    first = (n == 0).astype(jnp.int32)
    e = e_ref[...]
    qn = _headwise_rmsnorm(q_ref[...], e, qg_ref[...]).astype(BF16)
    kn = _headwise_rmsnorm(kv_ref[:, 0:KV_W], e, kg_ref[...])
    v = kv_ref[:, KV_W:2 * KV_W]
    ko_ref[...] = kn
    vo_ref[...] = v
    knb = kn.astype(BF16)
    vb = v.astype(BF16)

    low_q = lax.broadcasted_iota(jnp.int32, (2 * WINDOW, LANES), 1) < HEAD_DIM
    ones_half = (jnp.where(low_q, 1.0, 0.0).astype(BF16), jnp.where(low_q, 0.0, 1.0).astype(BF16))
    keep_lo, keep_hi = ones_half
    tiles = [(g, half) for g in range(N_KV) for half in range(2)]

    for i, (g, half) in enumerate(tiles):
        if half == 0:
            lanes = slice((g // 2) * LANES, (g // 2 + 1) * LANES)
            k_slab = jnp.concatenate([kp_ref[:, lanes], knb[:, lanes]], axis=0)
            v_slab = jnp.concatenate([vp_ref[:, lanes], vb[:, lanes]], axis=0)
            k_swap = pltpu.roll(k_slab, HEAD_DIM, axis=1)
            v_swap = pltpu.roll(v_slab, HEAD_DIM, axis=1)
            if g % 2 == 0:
                k_half = (k_slab * keep_lo, k_swap * keep_hi)
                v_half = (v_slab * keep_lo, v_swap * keep_hi)
            else:
                k_half = (k_swap * keep_lo, k_slab * keep_hi)
                v_half = (v_swap * keep_lo, v_slab * keep_hi)
            s0 = 2 * g
            q2 = jnp.concatenate([qn[:, s0 * LANES:(s0 + 1) * LANES], qn[:, (s0 + 1) * LANES:(s0 + 2) * LANES]],
                                 axis=0)
        s_ref[i] = lax.dot_general(q2, k_half[half], (((1,), (1,)), ((), ())), preferred_element_type=F32)
        rhs_ref[i] = jnp.concatenate([v_half[half], ones_half[half]], axis=1)

    for i, (g, half) in enumerate(tiles):
        heads = (4 * g + half, 4 * g + 2 + half)
        t = N_HEADS // N_KV * g + 2 * half
        s = s_ref[i] + bias_ref[first, t:t + 2].reshape(2 * WINDOW, 2 * WINDOW)
        sink = jnp.concatenate([jnp.full((WINDOW, 1), sink_ref[h], F32) for h in heads], axis=0)
        m = jnp.maximum(jnp.max(s, axis=-1, keepdims=True), sink)
        p_ref[i] = jnp.exp(s - m).astype(BF16)
        es_ref[i] = jnp.broadcast_to(jnp.exp(sink - m), (2 * WINDOW, LANES))

    for g in range(N_KV):
        lo, hi = 2 * g, 2 * g + 1
        acc = (jnp.dot(p_ref[lo], rhs_ref[lo], preferred_element_type=F32)
               + jnp.dot(p_ref[hi], rhs_ref[hi], preferred_element_type=F32))
        denom = acc[:, LANES:] + jnp.where(low_q, es_ref[lo], es_ref[hi])
        out = (acc[:, 0:LANES] / denom).astype(y_ref.dtype)
        y_ref[:, lo * LANES:(lo + 1) * LANES] = out[0:WINDOW]
        y_ref[:, hi * LANES:(hi + 1) * LANES] = out[WINDOW:2 * WINDOW]

    kp_ref[...] = knb
    vp_ref[...] = vb


def _bias_slot_heads():
    per = N_HEADS // N_KV
    return [per * (t // per) + 2 * (t % 2) + (t % per) // 2 for t in range(N_HEADS)]


def attn_prompt(z, bsz, seq, q_norm_g, k_norm_g, sinks, bias):
    assert seq % WINDOW == 0
    nb = seq // WINDOW
    q_col = 6 * GROUP_W // GROUP_W
    kv_col = 7 * GROUP_W // (2 * KV_W)
    qg = jnp.tile(q_norm_g, (1, N_HEADS)) * HEAD_DIM ** -0.5
    kg = jnp.tile(k_norm_g, (1, N_KV))
    e = jnp.asarray(np.kron(np.eye(MXU_COLS // HEAD_DIM), np.ones((HEAD_DIM, HEAD_DIM))), BF16)
    win = pl.BlockSpec((None, WINDOW, KV_W), lambda b, n: (b, 0, 0))
    full = lambda a: pl.BlockSpec(a.shape, lambda b, n: (0,) * a.ndim)
    return pl.pallas_call(
        _attn_prompt_kernel,
        out_shape=(
            jax.ShapeDtypeStruct((bsz * seq, GROUP_W), BF16),
            jax.ShapeDtypeStruct((bsz, WINDOW, KV_W), F32),
            jax.ShapeDtypeStruct((bsz, WINDOW, KV_W), F32),
        ),
        grid=(bsz, nb),
        in_specs=[
            pl.BlockSpec(memory_space=pltpu.SMEM),
            pl.BlockSpec((WINDOW, GROUP_W), lambda b, n: (b * nb + n, q_col)),
            pl.BlockSpec((WINDOW, 2 * KV_W), lambda b, n: (b * nb + n, kv_col)),
            full(qg), full(kg), full(e), full(bias),
        ],
        out_specs=(pl.BlockSpec((WINDOW, GROUP_W), lambda b, n: (b * nb + n, 0)), win, win),
        scratch_shapes=[
            pltpu.VMEM((WINDOW, KV_W), BF16), pltpu.VMEM((WINDOW, KV_W), BF16),
            pltpu.VMEM((2 * N_KV, 2 * WINDOW, 2 * WINDOW), F32),
            pltpu.VMEM((2 * N_KV, 2 * WINDOW, 2 * LANES), BF16),
            pltpu.VMEM((2 * N_KV, 2 * WINDOW, 2 * WINDOW), BF16),
            pltpu.VMEM((2 * N_KV, 2 * WINDOW, LANES), F32),
        ],
        compiler_params=_params("parallel", "arbitrary"),
        name="attn_prompt",
    )(sinks, z, z, qg, kg, e, bias)


def _mix_sample_kernel(sink_ref, *refs, start_pos):
    seq_in, shared, seq_out = refs[:6], refs[6:16], refs[16:]
    nseq = seq_in[0].shape[0]
    views = [tuple(r.at[s] for r in seq_in) + tuple(r.at[s] for r in seq_out) for s in range(nseq)]
    for z_ref, sa_ref, sp_ref, sc_ref, _, _, y_ref, sao_ref, spo_ref, sco_ref, _, _ in views:
        _mix_sample_abc(z_ref, sa_ref, sp_ref, sc_ref, *shared[:7], y_ref, sao_ref, spo_ref, sco_ref,
                        start_pos=start_pos)
    _mix_sample_attn(sink_ref, [(v[0], v[4], v[5], v[6], v[10], v[11]) for v in views], *shared[7:])


def _mix_sample_attn(sink_ref, views, qg_ref, kg_ref, bias_ref):
    gw = GROUP_W
    items = [(v, g) for v in views for g in range(N_KV)]
    scored = []
    for (z_ref, kb_ref, vb_ref, y_ref, ko_ref, vo_ref), g in items:
        wb = kb_ref.shape[0]
        zc = lambda c, w: z_ref[:, c:c + w]
        lanes = slice(g * HEAD_DIM, (g + 1) * HEAD_DIM)
        kn = _head_rmsnorm(zc(7 * gw + g * HEAD_DIM, HEAD_DIM), kg_ref[...])
        vn = zc(7 * gw + KV_W + g * HEAD_DIM, HEAD_DIM)
        ko_ref[wb - 1:wb, lanes] = kn
        vo_ref[wb - 1:wb, lanes] = vn
        qs, sinks = [], []
        for j in range(Q_PER_KV):
            h = g * Q_PER_KV + j
            qs.append(_head_rmsnorm(zc(6 * gw + h * HEAD_DIM, HEAD_DIM), qg_ref[...], HEAD_DIM ** -0.5))
            sinks.append(jnp.full((1, 1), sink_ref[h], F32))
        qcat = jnp.concatenate(qs, axis=0)
        sink = jnp.concatenate(sinks, axis=0)
        hb = bias_ref[g * Q_PER_KV:(g + 1) * Q_PER_KV, :]
        s_buf = lax.dot_general(qcat.astype(BF16), kb_ref[:, lanes].astype(BF16), (((1,), (1,)), ((), ())),
                                preferred_element_type=F32) + hb[:, WINDOW - wb:WINDOW]
        s_new = jnp.sum(qcat * kn, axis=-1, keepdims=True) + hb[:, WINDOW:WINDOW + 1]
        scored.append((s_buf, s_new, sink, vn))
    weighted = []
    for s_buf, s_new, sink, vn in scored:
        m = jnp.maximum(jnp.maximum(jnp.max(s_buf, axis=-1, keepdims=True), s_new), sink)
        p_buf = jnp.exp(s_buf - m)
        p_new = jnp.exp(s_new - m)
        denom = jnp.sum(p_buf, axis=-1, keepdims=True) + p_new + jnp.exp(sink - m)
        weighted.append(((p_buf / denom).astype(BF16), (p_new / denom) * vn))
    for ((z_ref, kb_ref, vb_ref, y_ref, ko_ref, vo_ref), g), (p_buf, o_new) in zip(items, weighted):
        lanes = slice(g * HEAD_DIM, (g + 1) * HEAD_DIM)
        o = jnp.dot(p_buf, vb_ref[:, lanes].astype(BF16), preferred_element_type=F32) + o_new
        for j in range(Q_PER_KV):
            h = g * Q_PER_KV + j
            y_ref[:, 3 * gw + h * HEAD_DIM:3 * gw + (h + 1) * HEAD_DIM] = o[j:j + 1, :].astype(y_ref.dtype)
    for _, kb_ref, vb_ref, _, ko_ref, vo_ref in views:
        wb = kb_ref.shape[0]
        ko_ref[0:wb - 1, :] = kb_ref[1:wb, :]
        vo_ref[0:wb - 1, :] = vb_ref[1:wb, :]


def _mix_sample_abc(z_ref, sa_ref, sp_ref, sc_ref, wa_ref, wp_ref, ps_ref, wc_ref, cb_ref, lg_ref, lb_ref,
                    y_ref, sao_ref, spo_ref, sco_ref, *, start_pos):
    gw = GROUP_W
    zc = lambda c, w=gw: z_ref[:, c:c + w]

    ua = zc(2 * gw) * zc(0)
    conv = wa_ref[0:1, :] * sa_ref[0:1, :] + wa_ref[1:2, :] * sa_ref[1:2, :] + wa_ref[2:3, :] * ua
    y_ref[:, 0:gw] = (zc(gw) * conv).astype(y_ref.dtype)
    sao_ref[0:1, :] = sa_ref[1:2, :]
    sao_ref[1:2, :] = ua

    ub = zc(3 * gw)
    for gi, win in enumerate(POOL_WINDOWS):
        lanes = slice(gi * POOL_GROUP_W, (gi + 1) * POOL_GROUP_W)
        u = ub[:, lanes]
        s = u + jnp.sum(sp_ref[POOL_STATE - (win - 1):POOL_STATE, lanes], axis=0, keepdims=True)
        cnt = float(min(start_pos + 1, win))
        d = s / cnt - u
        yb = jnp.dot(d.astype(BF16), wp_ref[gi], preferred_element_type=F32) * ps_ref[:, lanes]
        y_ref[:, gw + gi * POOL_GROUP_W:gw + (gi + 1) * POOL_GROUP_W] = yb.astype(y_ref.dtype)
    spo_ref[0:POOL_STATE - 1, :] = sp_ref[1:POOL_STATE, :]
    spo_ref[POOL_STATE - 1:POOL_STATE, :] = ub

    uc = zc(4 * gw) * jax.nn.sigmoid(zc(5 * gw))
    kc = CONV_C_W - 1
    acc = jnp.sum(wc_ref[0:kc, :] * sc_ref[...], axis=0, keepdims=True) + wc_ref[kc:kc + 1, :] * uc
    yc = _layer_norm_silu(acc + cb_ref[...], lg_ref[...], lb_ref[...])
    y_ref[:, 2 * gw:3 * gw] = yc.astype(y_ref.dtype)
    sco_ref[0:kc - 1, :] = sc_ref[1:kc, :]
    sco_ref[kc - 1:kc, :] = uc


def mix_sample(z, sa, sp, sc, kb, vb, start_pos, conv_a_w, pool_w, pool_scale, conv_c_w, conv_c_b, ln_g, ln_b,
               q_norm_g, k_norm_g, sinks, bias_row):
    nseq, in_cols = z.shape
    wb = kb.shape[1]
    spb = math.gcd(nseq, SAMPLE_SEQS)
    assert wb == WINDOW
    per_seq = lambda a: pl.BlockSpec((spb,) + a.shape[1:], lambda s: (s,) + (0,) * (a.ndim - 1))
    full = lambda a: pl.BlockSpec(a.shape, lambda s: (0,) * a.ndim)
    z3 = z.reshape(nseq, 1, in_cols)
    outs = pl.pallas_call(
        functools.partial(_mix_sample_kernel, start_pos=start_pos),
        out_shape=(
            jax.ShapeDtypeStruct((nseq, 1, 4 * GROUP_W), BF16),
            jax.ShapeDtypeStruct(sa.shape, F32), jax.ShapeDtypeStruct(sp.shape, F32),
            jax.ShapeDtypeStruct(sc.shape, F32), jax.ShapeDtypeStruct(kb.shape, F32),
            jax.ShapeDtypeStruct(vb.shape, F32),
        ),
        grid=(nseq // spb,),
        in_specs=[pl.BlockSpec(memory_space=pltpu.SMEM)] + [per_seq(a) for a in (z3, sa, sp, sc, kb, vb)] + [
            full(a) for a in (conv_a_w, pool_w, pool_scale, conv_c_w, conv_c_b, ln_g, ln_b,
                              q_norm_g, k_norm_g, bias_row)],
        out_specs=tuple(per_seq(a) for a in (jax.ShapeDtypeStruct((nseq, 1, 4 * GROUP_W), BF16), sa, sp, sc, kb, vb)),
        compiler_params=_params("parallel"),
        name="mix_sample",
    )(sinks, z3, sa, sp, sc, kb, vb, conv_a_w, pool_w, pool_scale, conv_c_w, conv_c_b, ln_g, ln_b,
      q_norm_g, k_norm_g, bias_row)
    return (outs[0].reshape(nseq, 4 * GROUP_W),) + tuple(outs[1:])


def kernel(x_prompt, x_sample, state_conv_a, state_pool, state_conv_c, cache_k_win, cache_v_win, rel_bias,
           norm_mix_g, w_in, conv_a_w, pool_w, pool_scale, conv_c_w, conv_c_b, ln_c_g, ln_c_b, q_norm_g, k_norm_g,
           attn_sinks, w_out, norm_ffn_g, w_gate, w_up, w_down):
    bsz, seq, d_model = x_prompt.shape
    nseq = x_sample.shape[0]
    depth = w_in.shape[0]
    wb = cache_k_win.shape[2]

    pool_w_b = pool_w.astype(BF16)
    bias = window_bias(rel_bias)
    bias_row = bias[0, np.argsort(_bias_slot_heads()), 0, :]
    kbuf = cache_k_win.reshape(depth, nseq, wb, KV_W)
    vbuf = cache_v_win.reshape(depth, nseq, wb, KV_W)
    row = lambda a, l: a[l][None, :]
    mixer_w = [(conv_a_w[l], pool_w_b[l], row(pool_scale, l), conv_c_w[l], row(conv_c_b, l),
                row(ln_c_g, l), row(ln_c_b, l)) for l in range(depth)]

    xs = x_sample.reshape(nseq, d_model)
    xp = x_prompt.reshape(bsz * seq, d_model)
    new_s = [[] for _ in range(5)]
    new_p = [[] for _ in range(5)]
    hb, ss = norm_prep(xp, row(norm_mix_g, 0), tm=512)
    w_in_b = w_out_b = w_gate_b = w_up_b = w_down_b = None
    for l in range(depth):
        cast = l if l == 0 else None
        stacked = lambda w, wb_: w if l == 0 else wb_
        zs, *made = norm_matmul_small(xs, row(norm_mix_g, l), stacked(w_in, w_in_b), cast, tn=512)
        w_in_b = made[0] if made else w_in_b
        y_s, sa_s, sp_s, sc_s, k_s, v_s = mix_sample(
            zs, state_conv_a[l], state_pool[l], state_conv_c[l], kbuf[l], vbuf[l], PAST_LEN,
            *mixer_w[l], row(q_norm_g, l), row(k_norm_g, l), attn_sinks[l], bias_row)
        xs, *made = resid_matmul_small(y_s, stacked(w_out, w_out_b), cast, xs, tn=512)
        w_out_b = made[0] if made else w_out_b
        ffs, *made = norm_gateup_small(xs, row(norm_ffn_g, l), stacked(w_gate, w_gate_b), stacked(w_up, w_up_b),
                                       cast, tn=256)
        w_gate_b, w_up_b = made if made else (w_gate_b, w_up_b)
        xs, *made = resid_matmul_small(ffs, stacked(w_down, w_down_b), cast, xs, tn=256)
        w_down_b = made[0] if made else w_down_b
        for i, a in enumerate((sa_s, sp_s, sc_s, k_s.reshape(nseq, wb, N_KV, HEAD_DIM),
                               v_s.reshape(nseq, wb, N_KV, HEAD_DIM))):
            new_s[i].append(a)

        more = l + 1 < depth
        side = lambda *ws: tuple((w, l + 1, swap) for w, swap in ws) if more else ()
        zp, *nxt_in = scaled_matmul(hb, ss, w_in_b, side((w_in, False)), tm=1024, tn=768)
        y_abc, sa_p, sp_p, sc_p = mix_abc_prompt(zp, bsz, seq, *mixer_w[l], tb=256)
        y_d, k_p, v_p = attn_prompt(zp, bsz, seq, row(q_norm_g, l), row(k_norm_g, l), attn_sinks[l], bias)
        xp, hb, ss, *nxt_out = resid_matmul((y_abc, y_d), w_out_b, xp, row(norm_ffn_g, l), side((w_out, False)),
                                            tm=1024, tn=512)
        ff, *nxt_ffn = scaled_gateup(hb, ss, w_gate_b, w_up_b, side((w_gate, False), (w_up, False), (w_down, True)),
                                     tm=2048, tn=256)
        if more:
            xp, hb, ss = resid_matmul((ff,), w_down_b, xp, row(norm_mix_g, l + 1), tm=512, tn=512)
            (w_in_b,), (w_out_b,), (w_gate_b, w_up_b, w_down_b) = nxt_in, nxt_out, nxt_ffn
        else:
            xp, = resid_matmul((ff,), w_down_b, xp, tm=512, tn=512)
        for i, a in enumerate((sa_p, sp_p, sc_p, k_p.reshape(bsz, WINDOW, N_KV, HEAD_DIM),
                               v_p.reshape(bsz, WINDOW, N_KV, HEAD_DIM))):
            new_p[i].append(a)

    return (xp.reshape(bsz, seq, d_model), xs.reshape(nseq, 1, d_model),
            *(jnp.stack(a) for a in new_p), *(jnp.stack(a) for a in new_s))
```

```python
import functools
import math

import numpy as np
import jax
import jax.numpy as jnp
from jax import lax
from jax.experimental import pallas as pl
from jax.experimental.pallas import tpu as pltpu

F32 = jnp.float32
BF16 = jnp.bfloat16

GROUP_W = 1024
HEAD_DIM = 64
N_HEADS = GROUP_W // HEAD_DIM
N_KV = 4
Q_PER_KV = N_HEADS // N_KV
KV_W = N_KV * HEAD_DIM
WINDOW = 128
CONV_A_W = 3
CONV_C_W = 31
POOL_WINDOWS = (2, 4, 8, 16)
POOL_GROUP_W = GROUP_W // len(POOL_WINDOWS)
POOL_STATE = max(POOL_WINDOWS) - 1
NUM_BUCKETS = 32
MAX_DISTANCE = 128
EPS = 1e-6
NEG_INF = float("-inf")

PAST_LEN = 8192

LANES = 128
SUBLANES = 8
SAMPLE_SEQS = 4

TILES = {
    "norm_prep": dict(tm=512),
    "in_proj": dict(tm=1024, tn=768),
    "out_proj": dict(tm=1024, tn=512),
    "gate_up": dict(tm=2048, tn=256),
    "down_proj": dict(tm=512, tn=512),
    "sample_in_proj": dict(tn=512),
    "sample_out_proj": dict(tn=512),
    "sample_gate_up": dict(tn=256),
    "sample_down_proj": dict(tn=256),
    "mix_abc": dict(tb=256),
}
MIX_CHUNK = 64
MXU_COLS = 256
HALO = 32
NORM_ROWS = 16
VMEM_LIMIT = 56 << 20


def _params(*sem):
    return pltpu.CompilerParams(dimension_semantics=sem, vmem_limit_bytes=VMEM_LIMIT)


def _rmsnorm_to(x_ref, g_ref, h_ref):
    rows = x_ref.shape[0]
    step = min(NORM_ROWS, rows)

    def body(i, carry):
        r = pl.multiple_of(i * step, step)
        x = x_ref[pl.ds(r, step), :]
        ms = jnp.mean(x * x, axis=-1, keepdims=True)
        h_ref[pl.ds(r, step), :] = (x * lax.rsqrt(ms + EPS) * g_ref[...]).astype(h_ref.dtype)
        return carry

    lax.fori_loop(0, rows // step, body, 0)


def _sample_w_spec(w, layer, tn):
    if layer is None:
        return pl.BlockSpec((w.shape[0], tn), lambda j: (0, j))
    return pl.BlockSpec((None, w.shape[1], tn), lambda j: (layer, 0, j))


def _bf16_tile(w_ref, wb_ref):
    if wb_ref is None:
        return w_ref[...]
    wb_ref[...] = w_ref[...].astype(BF16)
    return wb_ref[...]


def _norm_mm_small_kernel(x_ref, g_ref, w_ref, o_ref, *rest):
    wb_ref, h_ref = rest if len(rest) == 2 else (None, rest[0])

    @pl.when(pl.program_id(0) == 0)
    def _():
        _rmsnorm_to(x_ref, g_ref, h_ref)

    o_ref[...] = jnp.dot(h_ref[...], _bf16_tile(w_ref, wb_ref), preferred_element_type=F32)


def norm_matmul_small(x, g, w, layer, *, tn):
    m, d = x.shape
    n = w.shape[-1]
    assert n % tn == 0
    out_shape = [jax.ShapeDtypeStruct((m, n), F32)]
    out_specs = [pl.BlockSpec((m, tn), lambda j: (0, j))]
    if layer is not None:
        out_shape.append(jax.ShapeDtypeStruct((d, n), BF16))
        out_specs.append(pl.BlockSpec((d, tn), lambda j: (0, j)))
    return pl.pallas_call(
        _norm_mm_small_kernel,
        out_shape=tuple(out_shape),
        grid=(n // tn,),
        in_specs=[pl.BlockSpec((m, d), lambda j: (0, 0)), pl.BlockSpec((1, d), lambda j: (0, 0)),
                  _sample_w_spec(w, layer, tn)],
        out_specs=tuple(out_specs),
        scratch_shapes=[pltpu.VMEM((m, d), BF16)],
        compiler_params=_params("arbitrary"),
        name="sample_in_proj",
    )(x, g, w)


def _norm_gateup_small_kernel(x_ref, g_ref, wg_ref, wu_ref, o_ref, *rest):
    wgb_ref, wub_ref, h_ref = rest if len(rest) == 3 else (None, None, rest[0])

    @pl.when(pl.program_id(0) == 0)
    def _():
        _rmsnorm_to(x_ref, g_ref, h_ref)

    h = h_ref[...]
    a = jnp.dot(h, _bf16_tile(wg_ref, wgb_ref), preferred_element_type=F32)
    b = jnp.dot(h, _bf16_tile(wu_ref, wub_ref), preferred_element_type=F32)
    o_ref[...] = (a * jax.nn.sigmoid(a) * b).astype(o_ref.dtype)


def norm_gateup_small(x, g, w_gate, w_up, layer, *, tn):
    m, d = x.shape
    f = w_gate.shape[-1]
    assert f % tn == 0
    out_shape = [jax.ShapeDtypeStruct((m, f), BF16)]
    out_specs = [pl.BlockSpec((m, tn), lambda j: (0, j))]
    if layer is not None:
        out_shape += [jax.ShapeDtypeStruct((d, f), BF16)] * 2
        out_specs += [pl.BlockSpec((d, tn), lambda j: (0, j))] * 2
    return pl.pallas_call(
        _norm_gateup_small_kernel,
        out_shape=tuple(out_shape),
        grid=(f // tn,),
        in_specs=[pl.BlockSpec((m, d), lambda j: (0, 0)), pl.BlockSpec((1, d), lambda j: (0, 0)),
                  _sample_w_spec(w_gate, layer, tn), _sample_w_spec(w_up, layer, tn)],
        out_specs=tuple(out_specs),
        scratch_shapes=[pltpu.VMEM((m, d), BF16)],
        compiler_params=_params("arbitrary"),
        name="sample_gate_up",
    )(x, g, w_gate, w_up)


def _resid_mm_small_kernel(a_ref, w_ref, r_ref, o_ref, wb_ref=None):
    o_ref[...] = r_ref[...] + jnp.dot(a_ref[...], _bf16_tile(w_ref, wb_ref), preferred_element_type=F32)


def resid_matmul_small(a, w, layer, resid, *, tn):
    m, n = resid.shape
    k = a.shape[1]
    assert w.shape[-2] == k and n % tn == 0
    out_shape = [jax.ShapeDtypeStruct((m, n), F32)]
    out_specs = [pl.BlockSpec((m, tn), lambda j: (0, j))]
    if layer is not None:
        out_shape.append(jax.ShapeDtypeStruct((k, n), BF16))
        out_specs.append(pl.BlockSpec((k, tn), lambda j: (0, j)))
    return pl.pallas_call(
        _resid_mm_small_kernel,
        out_shape=tuple(out_shape),
        grid=(n // tn,),
        in_specs=[pl.BlockSpec((m, k), lambda j: (0, 0)), _sample_w_spec(w, layer, tn),
                  pl.BlockSpec((m, tn), lambda j: (0, j))],
        out_specs=tuple(out_specs),
        compiler_params=_params("arbitrary"),
        name="sample_proj_residual",
    )(a, w, resid)


def _sum_lane_groups(x):
    acc = x[:, 0:LANES]
    for k in range(1, x.shape[1] // LANES):
        acc = acc + x[:, k * LANES:(k + 1) * LANES]
    return acc


def _rowscale_to(ss_ref, rs_ref, d):
    tot = jnp.sum(_sum_lane_groups(ss_ref[...]), axis=-1, keepdims=True)
    rs_ref[...] = jnp.broadcast_to(lax.rsqrt(tot / d + EPS), rs_ref.shape)


def _norm_prep_kernel(x_ref, g_ref, hb_ref, ss_ref):
    rows = x_ref.shape[0]

    def body(i, carry):
        r = pl.multiple_of(i * NORM_ROWS, NORM_ROWS)
        x = x_ref[pl.ds(r, NORM_ROWS), :]
        hb_ref[pl.ds(r, NORM_ROWS), :] = (x * g_ref[...]).astype(BF16)
        ss_ref[pl.ds(r, NORM_ROWS), :] = _sum_lane_groups(x * x)
        return carry

    lax.fori_loop(0, rows // NORM_ROWS, body, 0)


def norm_prep(x, g, *, tm):
    m, d = x.shape
    assert m % tm == 0 and tm % NORM_ROWS == 0
    return pl.pallas_call(
        _norm_prep_kernel,
        out_shape=(jax.ShapeDtypeStruct((m, d), BF16), jax.ShapeDtypeStruct((m, LANES), F32)),
        grid=(m // tm,),
        in_specs=[pl.BlockSpec((tm, d), lambda i: (i, 0)), pl.BlockSpec((1, d), lambda i: (0, 0))],
        out_specs=(pl.BlockSpec((tm, d), lambda i: (i, 0)), pl.BlockSpec((tm, LANES), lambda i: (i, 0))),
        compiler_params=_params("parallel"),
        name="norm_prep",
    )(x, g)


def _side_cast_specs(side_casts, gm, gn):
    in_specs, out_specs, out_shape, args = [], [], [], []
    for w, layer, swap in side_casts:
        _, r, c = w.shape
        gr, gc = (gn, gm) if swap else (gm, gn)
        assert r % (gr * 2 * SUBLANES) == 0 and c % (gc * LANES) == 0
        block = (r // gr, c // gc)
        pick = (lambda i, j: (j, i)) if swap else (lambda i, j: (i, j))
        in_specs.append(pl.BlockSpec((None,) + block, lambda i, j, pick=pick, layer=layer: (layer,) + pick(i, j)))
        out_specs.append(pl.BlockSpec(block, pick))
        out_shape.append(jax.ShapeDtypeStruct((r, c), BF16))
        args.append(w)
    return in_specs, out_specs, out_shape, args


def _run_side_casts(src_refs, dst_refs):
    for src, dst in zip(src_refs, dst_refs):
        dst[...] = src[...].astype(BF16)


def _scaled_mm_kernel(hb_ref, ss_ref, w_ref, *refs):
    n_side = (len(refs) - 2) // 2
    o_ref, rs_ref = refs[n_side], refs[-1]

    @pl.when(pl.program_id(1) == 0)
    def _():
        _rowscale_to(ss_ref, rs_ref, hb_ref.shape[1])

    _run_side_casts(refs[:n_side], refs[n_side + 1:-1])
    acc = jnp.dot(hb_ref[...], w_ref[...], preferred_element_type=F32)
    rs = rs_ref[...]
    for k in range(o_ref.shape[1] // LANES):
        o_ref[:, k * LANES:(k + 1) * LANES] = acc[:, k * LANES:(k + 1) * LANES] * rs


def scaled_matmul(hb, ss, w, side_casts=(), *, tm, tn):
    m, d = hb.shape
    n = w.shape[1]
    assert m % tm == 0 and n % tn == 0 and tn % LANES == 0
    s_in, s_out, s_shape, s_args = _side_cast_specs(side_casts, m // tm, n // tn)
    return pl.pallas_call(
        _scaled_mm_kernel,
        out_shape=(jax.ShapeDtypeStruct((m, n), F32), *s_shape),
        grid=(m // tm, n // tn),
        in_specs=[
            pl.BlockSpec((tm, d), lambda i, j: (i, 0)),
            pl.BlockSpec((tm, ss.shape[1]), lambda i, j: (i, 0)),
            pl.BlockSpec((d, tn), lambda i, j: (0, j)),
            *s_in,
        ],
        out_specs=(pl.BlockSpec((tm, tn), lambda i, j: (i, j)), *s_out),
        scratch_shapes=[pltpu.VMEM((tm, LANES), F32)],
        compiler_params=_params("parallel", "arbitrary"),
        name="in_proj",
    )(hb, ss, w, *s_args)


def _scaled_gateup_kernel(hb_ref, ss_ref, wg_ref, wu_ref, *refs):
    n_side = (len(refs) - 2) // 2
    o_ref, rs_ref = refs[n_side], refs[-1]

    @pl.when(pl.program_id(1) == 0)
    def _():
        _rowscale_to(ss_ref, rs_ref, hb_ref.shape[1])

    _run_side_casts(refs[:n_side], refs[n_side + 1:-1])
    half = hb_ref.shape[0] // 2
    for r0 in (0, half):
        rows = slice(r0, r0 + half)
        h = hb_ref[rows, :]
        a = jnp.dot(h, wg_ref[...], preferred_element_type=F32)
        b = jnp.dot(h, wu_ref[...], preferred_element_type=F32)
        rs = rs_ref[rows, :]
        for k in range(o_ref.shape[1] // LANES):
            cols = slice(k * LANES, (k + 1) * LANES)
            ak = a[:, cols] * rs
            o_ref[rows, cols] = (ak * jax.nn.sigmoid(ak) * (b[:, cols] * rs)).astype(o_ref.dtype)


def scaled_gateup(hb, ss, w_gate, w_up, side_casts=(), *, tm, tn):
    m, d = hb.shape
    f = w_gate.shape[1]
    assert m % tm == 0 and f % tn == 0 and tn % LANES == 0
    w_spec = pl.BlockSpec((d, tn), lambda i, j: (0, j))
    s_in, s_out, s_shape, s_args = _side_cast_specs(side_casts, m // tm, f // tn)
    return pl.pallas_call(
        _scaled_gateup_kernel,
        out_shape=(jax.ShapeDtypeStruct((m, f), BF16), *s_shape),
        grid=(m // tm, f // tn),
        in_specs=[
            pl.BlockSpec((tm, d), lambda i, j: (i, 0), pipeline_mode=pl.Buffered(1)),
            pl.BlockSpec((tm, ss.shape[1]), lambda i, j: (i, 0)),
            w_spec, w_spec, *s_in],
        out_specs=(pl.BlockSpec((tm, tn), lambda i, j: (i, j)), *s_out),
        scratch_shapes=[pltpu.VMEM((tm, LANES), F32)],
        compiler_params=_params("parallel", "arbitrary"),
        name="gate_up",
    )(hb, ss, w_gate, w_up, *s_args)


def _resid_mm_kernel(*refs, k_splits, emit_norm, n_side):
    n_a = len(k_splits)
    a_refs = refs[:n_a]
    n_in = 3 if emit_norm else 2
    n_out = 3 if emit_norm else 1
    main_in = refs[n_a:n_a + n_in]
    side_in = refs[n_a + n_in:n_a + n_in + n_side]
    main_out = refs[n_a + n_in + n_side:n_a + n_in + n_side + n_out]
    side_out = refs[n_a + n_in + n_side + n_out:]
    if emit_norm:
        w_ref, r_ref, g_ref = main_in
        o_ref, hb_ref, ss_ref = main_out
    else:
        w_ref, r_ref = main_in
        o_ref, = main_out
    _run_side_casts(side_in, side_out)
    if emit_norm:
        @pl.when(pl.program_id(1) == 0)
        def _():
            ss_ref[...] = jnp.zeros_like(ss_ref)

    tn = o_ref.shape[1]
    part = None
    for c in range(0, tn, MXU_COLS):
        cols = slice(c, c + MXU_COLS)
        acc = r_ref[:, cols]
        off = 0
        for a_ref, k in zip(a_refs, k_splits):
            acc = acc + jnp.dot(a_ref[...], w_ref[off:off + k, cols], preferred_element_type=F32)
            off += k
        o_ref[:, cols] = acc
        if emit_norm:
            hb_ref[:, cols] = (acc * g_ref[:, cols]).astype(BF16)
            ss = _sum_lane_groups(acc * acc)
            part = ss if part is None else part + ss
    if emit_norm:
        ss_ref[...] += part


def resid_matmul(acts, w, resid, g_next=None, side_casts=(), *, tm, tn):
    m, n = resid.shape
    k_splits = tuple(a.shape[1] for a in acts)
    k_total = sum(k_splits)
    emit_norm = g_next is not None
    assert w.shape[0] == k_total and m % tm == 0 and n % tn == 0 and tn % MXU_COLS == 0
    tile = pl.BlockSpec((tm, tn), lambda i, j: (i, j))
    in_specs = [pl.BlockSpec((tm, k), lambda i, j: (i, 0)) for k in k_splits] + [
        pl.BlockSpec((k_total, tn), lambda i, j: (0, j)), tile]
    out_shape = [jax.ShapeDtypeStruct((m, n), F32)]
    out_specs = [tile]
    args = (*acts, w, resid)
    if emit_norm:
        in_specs.append(pl.BlockSpec((1, tn), lambda i, j: (0, j)))
        out_shape += [jax.ShapeDtypeStruct((m, n), BF16), jax.ShapeDtypeStruct((m, LANES), F32)]
        out_specs += [tile, pl.BlockSpec((tm, LANES), lambda i, j: (i, 0))]
        args = args + (g_next,)
    s_in, s_out, s_shape, s_args = _side_cast_specs(side_casts, m // tm, n // tn)
    return pl.pallas_call(
        functools.partial(_resid_mm_kernel, k_splits=k_splits, emit_norm=emit_norm, n_side=len(s_args)),
        out_shape=(*out_shape, *s_shape),
        grid=(m // tm, n // tn),
        in_specs=in_specs + s_in,
        out_specs=(*out_specs, *s_out),
        compiler_params=_params("parallel", "arbitrary"),
        name="proj_residual",
    )(*args, *s_args)


def _bucket_matrix():
    i = np.arange(WINDOW)[:, None]
    j = np.arange(2 * WINDOW)[None, :]
    rel = i + WINDOW - j
    max_exact = NUM_BUCKETS // 2
    nf = np.maximum(rel, 1).astype(np.float32)
    large = max_exact + (np.log(nf / max_exact) / math.log(MAX_DISTANCE / max_exact)
                         * (NUM_BUCKETS - max_exact)).astype(np.int32)
    large = np.minimum(large, NUM_BUCKETS - 1)
    bucket = np.where(rel < max_exact, rel, large)
    valid = (rel >= 0) & (rel < WINDOW)
    later = np.where(valid, bucket, -1).astype(np.int32)
    first = np.where(j >= WINDOW, later, -1)
    return np.stack([later, first])


def _bias_kernel(rb_ref, bk_ref, o_ref):
    t = pl.program_id(1)
    per = N_HEADS // N_KV
    h = per * (t // per) + 2 * (t % 2) + (t % per) // 2
    bk = bk_ref[...]
    acc = jnp.full(bk.shape, NEG_INF, F32)
    for b in range(NUM_BUCKETS):
        acc = jnp.where(bk == b, rb_ref[b, h], acc)
    o_ref[...] = acc


def window_bias(rel_bias):
    bucket = jnp.asarray(_bucket_matrix())
    return pl.pallas_call(
        _bias_kernel,
        out_shape=jax.ShapeDtypeStruct((2, N_HEADS, WINDOW, 2 * WINDOW), F32),
        grid=(2, N_HEADS),
        in_specs=[
            pl.BlockSpec(memory_space=pltpu.SMEM),
            pl.BlockSpec((None, WINDOW, 2 * WINDOW), lambda f, h: (f, 0, 0)),
        ],
        out_specs=pl.BlockSpec((None, None, WINDOW, 2 * WINDOW), lambda f, h: (f, h, 0, 0)),
        compiler_params=_params("arbitrary", "arbitrary"),
        name="window_bias",
    )(rel_bias, bucket)


def _layer_norm_silu(y, g, b):
    mu = jnp.mean(y, axis=-1, keepdims=True)
    var = jnp.mean(jnp.square(y - mu), axis=-1, keepdims=True)
    y = (y - mu) * lax.rsqrt(var + EPS) * g + b
    return y * jax.nn.sigmoid(y)


def _delay_rows(blocks, b):
    if b == 0:
        return blocks[1:]
    return [jnp.concatenate([blocks[i], blocks[i + 1]], axis=0)[SUBLANES - b:2 * SUBLANES - b]
            for i in range(len(blocks) - 1)]


def _trailing_sum(blocks, n):
    step = 1
    while step < n:
        blocks = [x + y for x, y in zip(blocks[1:], _delay_rows(blocks, step))]
        step *= 2
    return blocks


def _store_bf16_rows(ref, r0, cols, blocks):
    for p in range(len(blocks) // 2):
        pair = jnp.concatenate([blocks[2 * p], blocks[2 * p + 1]], axis=0)
        ref[pl.ds(r0 + 2 * SUBLANES * p, 2 * SUBLANES), cols] = pair.astype(ref.dtype)


def _mix_abc_kernel(ha_ref, ba_ref, ca_ref, ub_ref, ac_ref, gc_ref,
                    wa_ref, wp_ref, ps_ref, wc_ref, cb_ref, lg_ref, lb_ref,
                    y_ref, sa_ref, sp_ref, sc_ref,
                    ea_ref, eb_ref, ec_ref, d_ref, pre_ref, *, tb):
    t = pl.program_id(1)
    nt = pl.num_programs(1)
    nblk = MIX_CHUNK // SUBLANES
    hist = HALO // SUBLANES

    @pl.when(t == 0)
    def _():
        zeros = jnp.zeros((HALO, GROUP_W), F32)
        ea_ref[0:HALO, :] = zeros
        eb_ref[0:HALO, :] = zeros
        ec_ref[0:HALO, :] = zeros

    ea_ref[HALO:HALO + tb, :] = ca_ref[...] * ha_ref[...]
    eb_ref[HALO:HALO + tb, :] = ub_ref[...]
    ec_ref[HALO:HALO + tb, :] = ac_ref[...] * jax.nn.sigmoid(gc_ref[...])

    def chunk(ci, carry):
        r0 = pl.multiple_of(ci * MIX_CHUNK, MIX_CHUNK)
        load = lambda ref, j, cols: ref[pl.ds(r0 + SUBLANES * j, SUBLANES), cols]
        row_pos = t * tb + r0 + lax.broadcasted_iota(jnp.int32, (SUBLANES, LANES), 0)
        for c in range(GROUP_W // LANES):
            cols = slice(c * LANES, (c + 1) * LANES)

            xa = [load(ea_ref, j, cols) for j in range(hist - 1, hist + nblk)]
            conv = [wa_ref[CONV_A_W - 1:CONV_A_W, cols] * x for x in xa[1:]]
            for b in range(1, CONV_A_W):
                w = wa_ref[CONV_A_W - 1 - b:CONV_A_W - b, cols]
                conv = [x + y for x, y in zip(conv, _delay_rows([w * x for x in xa], b))]
            ya = [ba_ref[pl.ds(r0 + SUBLANES * i, SUBLANES), cols] * conv[i] for i in range(nblk)]
            _store_bf16_rows(y_ref, r0, cols, ya)

            win = POOL_WINDOWS[c * LANES // POOL_GROUP_W]
            xb = [load(eb_ref, j, cols) for j in range(hist + nblk)]
            if win > SUBLANES:
                assert win == 2 * SUBLANES
                sums = _trailing_sum([x + y for x, y in zip(xb[1:], xb[:-1])], SUBLANES)
            else:
                sums = _trailing_sum(xb, win)
            sums = sums[len(sums) - nblk:]
            db = []
            for i in range(nblk):
                cnt = jnp.minimum(row_pos + (SUBLANES * i + 1), win).astype(F32)
                db.append(sums[i] / cnt - xb[hist + i])
            _store_bf16_rows(d_ref, r0, cols, db)

            xc = [load(ec_ref, j, cols) for j in range(hist + nblk)]
            acc = None
            for b in range(SUBLANES):
                q = None
                for a in range(hist):
                    delay = SUBLANES * a + b
                    if delay > CONV_C_W - 1:
                        continue
                    w = wc_ref[CONV_C_W - 1 - delay:CONV_C_W - delay, cols]
                    terms = [w * xc[hist - 1 - a + i] for i in range(nblk + 1)]
                    q = terms if q is None else [x + y for x, y in zip(q, terms)]
                q = _delay_rows(q, b)
                acc = q if acc is None else [x + y for x, y in zip(acc, q)]
            for i in range(nblk):
                pre_ref[pl.ds(r0 + SUBLANES * i, SUBLANES), cols] = acc[i] + cb_ref[:, cols]
        return carry

    lax.fori_loop(0, tb // MIX_CHUNK, chunk, 0)

    for gi in range(len(POOL_WINDOWS)):
        lanes = slice(gi * POOL_GROUP_W, (gi + 1) * POOL_GROUP_W)
        yb = jnp.dot(d_ref[:, lanes], wp_ref[gi], preferred_element_type=F32) * ps_ref[:, lanes]
        y_ref[:, GROUP_W + gi * POOL_GROUP_W:GROUP_W + (gi + 1) * POOL_GROUP_W] = yb.astype(y_ref.dtype)

    yc = _layer_norm_silu(pre_ref[...], lg_ref[...], lb_ref[...])
    y_ref[:, 2 * GROUP_W:3 * GROUP_W] = yc.astype(y_ref.dtype)

    @pl.when(t == nt - 1)
    def _():
        end = HALO + tb
        sa_ref[...] = ea_ref[end - (CONV_A_W - 1):end, :]
        sp_ref[...] = eb_ref[end - POOL_STATE:end, :]
        sc_ref[...] = ec_ref[end - (CONV_C_W - 1):end, :]

    ea_ref[0:HALO, :] = ea_ref[tb:tb + HALO, :]
    eb_ref[0:HALO, :] = eb_ref[tb:tb + HALO, :]
    ec_ref[0:HALO, :] = ec_ref[tb:tb + HALO, :]


def mix_abc_prompt(z, bsz, seq, conv_a_w, pool_w, pool_scale, conv_c_w, conv_c_b, ln_g, ln_b, *, tb):
    assert seq % tb == 0 and tb >= HALO and tb % MIX_CHUNK == 0
    assert HALO % SUBLANES == 0 and HALO >= CONV_C_W - 1 and HALO >= 2 * SUBLANES >= max(POOL_WINDOWS)
    nt = seq // tb
    col = lambda c: pl.BlockSpec((tb, GROUP_W), lambda b, t: (b * nt + t, c))
    full = lambda a: pl.BlockSpec(a.shape, lambda b, t: (0,) * a.ndim)
    state = lambda rows: pl.BlockSpec((None, rows, GROUP_W), lambda b, t: (b, 0, 0))
    ext = pltpu.VMEM((HALO + tb, GROUP_W), F32)
    return pl.pallas_call(
        functools.partial(_mix_abc_kernel, tb=tb),
        out_shape=(
            jax.ShapeDtypeStruct((bsz * seq, 3 * GROUP_W), BF16),
            jax.ShapeDtypeStruct((bsz, CONV_A_W - 1, GROUP_W), F32),
            jax.ShapeDtypeStruct((bsz, POOL_STATE, GROUP_W), F32),
            jax.ShapeDtypeStruct((bsz, CONV_C_W - 1, GROUP_W), F32),
        ),
        grid=(bsz, nt),
        in_specs=[col(c) for c in range(6)] + [
            full(conv_a_w), full(pool_w), full(pool_scale), full(conv_c_w),
            full(conv_c_b), full(ln_g), full(ln_b)],
        out_specs=(
            pl.BlockSpec((tb, 3 * GROUP_W), lambda b, t: (b * nt + t, 0)),
            state(CONV_A_W - 1), state(POOL_STATE), state(CONV_C_W - 1),
        ),
        scratch_shapes=[ext, ext, ext, pltpu.VMEM((tb, GROUP_W), BF16), pltpu.VMEM((tb, GROUP_W), F32)],
        compiler_params=_params("parallel", "arbitrary"),
        name="mix_abc_prompt",
    )(z, z, z, z, z, z, conv_a_w, pool_w, pool_scale, conv_c_w, conv_c_b, ln_g, ln_b)


def _head_rmsnorm(x, g, scale=1.0):
    ms = jnp.mean(x * x, axis=-1, keepdims=True)
    return x * (lax.rsqrt(ms + EPS) * scale) * g


def _headwise_rmsnorm(x, e, g):
    x2 = x * x
    hi = x2.astype(BF16)
    lo = (x2 - hi.astype(F32)).astype(BF16)
    sums = []
    for c in range(0, x.shape[1], MXU_COLS):
        cols = slice(c, c + MXU_COLS)
        sums.append(jnp.dot(hi[:, cols], e, preferred_element_type=F32)
                    + jnp.dot(lo[:, cols], e, preferred_element_type=F32))
    ss = jnp.concatenate(sums, axis=1)
    return x * lax.rsqrt(ss * (1.0 / HEAD_DIM) + EPS) * g


def _attn_prompt_kernel(sink_ref, q_ref, kv_ref, qg_ref, kg_ref, e_ref, bias_ref,
                        y_ref, ko_ref, vo_ref, kp_ref, vp_ref, s_ref, rhs_ref, p_ref, es_ref):
    n = pl.program_id(1)

    @pl.when(n == 0)
    def _():
        kp_ref[...] = jnp.zeros_like(kp_ref)
        vp_ref[...] = jnp.zeros_like(vp_ref)

    first = (n == 0).astype(jnp.int32)
    e = e_ref[...]
    qn = _headwise_rmsnorm(q_ref[...], e, qg_ref[...]).astype(BF16)
    kn = _headwise_rmsnorm(kv_ref[:, 0:KV_W], e, kg_ref[...])
    v = kv_ref[:, KV_W:2 * KV_W]
    ko_ref[...] = kn
    vo_ref[...] = v
    knb = kn.astype(BF16)
    vb = v.astype(BF16)

    low_q = lax.broadcasted_iota(jnp.int32, (2 * WINDOW, LANES), 1) < HEAD_DIM
    ones_half = (jnp.where(low_q, 1.0, 0.0).astype(BF16), jnp.where(low_q, 0.0, 1.0).astype(BF16))
    keep_lo, keep_hi = ones_half
    tiles = [(g, half) for g in range(N_KV) for half in range(2)]

    for i, (g, half) in enumerate(tiles):
        if half == 0:
            lanes = slice((g // 2) * LANES, (g // 2 + 1) * LANES)
            k_slab = jnp.concatenate([kp_ref[:, lanes], knb[:, lanes]], axis=0)
            v_slab = jnp.concatenate([vp_ref[:, lanes], vb[:, lanes]], axis=0)
            k_swap = pltpu.roll(k_slab, HEAD_DIM, axis=1)
            v_swap = pltpu.roll(v_slab, HEAD_DIM, axis=1)
            if g % 2 == 0:
                k_half = (k_slab * keep_lo, k_swap * keep_hi)
                v_half = (v_slab * keep_lo, v_swap * keep_hi)
            else:
                k_half = (k_swap * keep_lo, k_slab * keep_hi)
                v_half = (v_swap * keep_lo, v_slab * keep_hi)
            s0 = 2 * g
            q2 = jnp.concatenate([qn[:, s0 * LANES:(s0 + 1) * LANES], qn[:, (s0 + 1) * LANES:(s0 + 2) * LANES]],
                                 axis=0)
        s_ref[i] = lax.dot_general(q2, k_half[half], (((1,), (1,)), ((), ())), preferred_element_type=F32)
        rhs_ref[i] = jnp.concatenate([v_half[half], ones_half[half]], axis=1)

    for i, (g, half) in enumerate(tiles):
        heads = (4 * g + half, 4 * g + 2 + half)
        t = N_HEADS // N_KV * g + 2 * half
        s = s_ref[i] + bias_ref[first, t:t + 2].reshape(2 * WINDOW, 2 * WINDOW)
        sink = jnp.concatenate([jnp.full((WINDOW, LANES), sink_ref[h], F32) for h in heads], axis=0)
        m = jnp.maximum(jnp.broadcast_to(jnp.max(s, axis=-1, keepdims=True), (2 * WINDOW, LANES)), sink)
        p_ref[i] = jnp.exp(s - jnp.concatenate([m, m], axis=1)).astype(BF16)
        es_ref[i] = jnp.exp(sink - m)

    for g in range(N_KV):
        lo, hi = 2 * g, 2 * g + 1
        acc = (jnp.dot(p_ref[lo], rhs_ref[lo], preferred_element_type=F32)
               + jnp.dot(p_ref[hi], rhs_ref[hi], preferred_element_type=F32))
        denom = acc[:, LANES:] + jnp.where(low_q, es_ref[lo], es_ref[hi])
        out = (acc[:, 0:LANES] / denom).astype(y_ref.dtype)
        y_ref[:, lo * LANES:(lo + 1) * LANES] = out[0:WINDOW]
        y_ref[:, hi * LANES:(hi + 1) * LANES] = out[WINDOW:2 * WINDOW]

    kp_ref[...] = knb
    vp_ref[...] = vb


def _bias_slot_heads():
    per = N_HEADS // N_KV
    return [per * (t // per) + 2 * (t % 2) + (t % per) // 2 for t in range(N_HEADS)]


def attn_prompt(z, bsz, seq, q_norm_g, k_norm_g, sinks, bias):
    assert seq % WINDOW == 0
    nb = seq // WINDOW
    q_col = 6 * GROUP_W // GROUP_W
    kv_col = 7 * GROUP_W // (2 * KV_W)
    qg = jnp.tile(q_norm_g, (1, N_HEADS)) * HEAD_DIM ** -0.5
    kg = jnp.tile(k_norm_g, (1, N_KV))
    e = jnp.asarray(np.kron(np.eye(MXU_COLS // HEAD_DIM), np.ones((HEAD_DIM, HEAD_DIM))), BF16)
    win = pl.BlockSpec((None, WINDOW, KV_W), lambda b, n: (b, 0, 0))
    full = lambda a: pl.BlockSpec(a.shape, lambda b, n: (0,) * a.ndim)
    return pl.pallas_call(
        _attn_prompt_kernel,
        out_shape=(
            jax.ShapeDtypeStruct((bsz * seq, GROUP_W), BF16),
            jax.ShapeDtypeStruct((bsz, WINDOW, KV_W), F32),
            jax.ShapeDtypeStruct((bsz, WINDOW, KV_W), F32),
        ),
        grid=(bsz, nb),
        in_specs=[
            pl.BlockSpec(memory_space=pltpu.SMEM),
            pl.BlockSpec((WINDOW, GROUP_W), lambda b, n: (b * nb + n, q_col)),
            pl.BlockSpec((WINDOW, 2 * KV_W), lambda b, n: (b * nb + n, kv_col)),
            full(qg), full(kg), full(e), full(bias),
        ],
        out_specs=(pl.BlockSpec((WINDOW, GROUP_W), lambda b, n: (b * nb + n, 0)), win, win),
        scratch_shapes=[
            pltpu.VMEM((WINDOW, KV_W), BF16), pltpu.VMEM((WINDOW, KV_W), BF16),
            pltpu.VMEM((2 * N_KV, 2 * WINDOW, 2 * WINDOW), F32),
            pltpu.VMEM((2 * N_KV, 2 * WINDOW, 2 * LANES), BF16),
            pltpu.VMEM((2 * N_KV, 2 * WINDOW, 2 * WINDOW), BF16),
            pltpu.VMEM((2 * N_KV, 2 * WINDOW, LANES), F32),
        ],
        compiler_params=_params("parallel", "arbitrary"),
        name="attn_prompt",
    )(sinks, z, z, qg, kg, e, bias)


def _mix_sample_kernel(sink_ref, *refs, start_pos):
    seq_in, shared, seq_out = refs[:6], refs[6:16], refs[16:]
    nseq = seq_in[0].shape[0]
    views = [tuple(r.at[s] for r in seq_in) + tuple(r.at[s] for r in seq_out) for s in range(nseq)]
    for z_ref, sa_ref, sp_ref, sc_ref, _, _, y_ref, sao_ref, spo_ref, sco_ref, _, _ in views:
        _mix_sample_abc(z_ref, sa_ref, sp_ref, sc_ref, *shared[:7], y_ref, sao_ref, spo_ref, sco_ref,
                        start_pos=start_pos)
    _mix_sample_attn(sink_ref, [(v[0], v[4], v[5], v[6], v[10], v[11]) for v in views], *shared[7:])


def _mix_sample_attn(sink_ref, views, qg_ref, kg_ref, bias_ref):
    gw = GROUP_W
    items = [(v, g) for v in views for g in range(N_KV)]
    scored = []
    for (z_ref, kb_ref, vb_ref, y_ref, ko_ref, vo_ref), g in items:
        wb = kb_ref.shape[0]
        zc = lambda c, w: z_ref[:, c:c + w]
        lanes = slice(g * HEAD_DIM, (g + 1) * HEAD_DIM)
        kn = _head_rmsnorm(zc(7 * gw + g * HEAD_DIM, HEAD_DIM), kg_ref[...])
        vn = zc(7 * gw + KV_W + g * HEAD_DIM, HEAD_DIM)
        ko_ref[wb - 1:wb, lanes] = kn
        vo_ref[wb - 1:wb, lanes] = vn
        qs, sinks = [], []
        for j in range(Q_PER_KV):
            h = g * Q_PER_KV + j
            qs.append(_head_rmsnorm(zc(6 * gw + h * HEAD_DIM, HEAD_DIM), qg_ref[...], HEAD_DIM ** -0.5))
            sinks.append(jnp.full((1, 1), sink_ref[h], F32))
        qcat = jnp.concatenate(qs, axis=0)
        sink = jnp.concatenate(sinks, axis=0)
        hb = bias_ref[g * Q_PER_KV:(g + 1) * Q_PER_KV, :]
        s_buf = lax.dot_general(qcat.astype(BF16), kb_ref[:, lanes].astype(BF16), (((1,), (1,)), ((), ())),
                                preferred_element_type=F32) + hb[:, WINDOW - wb:WINDOW]
        s_new = jnp.sum(qcat * kn, axis=-1, keepdims=True) + hb[:, WINDOW:WINDOW + 1]
        scored.append((s_buf, s_new, sink, vn))
    weighted = []
    for s_buf, s_new, sink, vn in scored:
        m = jnp.maximum(jnp.maximum(jnp.max(s_buf, axis=-1, keepdims=True), s_new), sink)
        p_buf = jnp.exp(s_buf - m)
        p_new = jnp.exp(s_new - m)
        denom = jnp.sum(p_buf, axis=-1, keepdims=True) + p_new + jnp.exp(sink - m)
        weighted.append(((p_buf / denom).astype(BF16), (p_new / denom) * vn))
    for ((z_ref, kb_ref, vb_ref, y_ref, ko_ref, vo_ref), g), (p_buf, o_new) in zip(items, weighted):
        lanes = slice(g * HEAD_DIM, (g + 1) * HEAD_DIM)
        o = jnp.dot(p_buf, vb_ref[:, lanes].astype(BF16), preferred_element_type=F32) + o_new
        for j in range(Q_PER_KV):
            h = g * Q_PER_KV + j
            y_ref[:, 3 * gw + h * HEAD_DIM:3 * gw + (h + 1) * HEAD_DIM] = o[j:j + 1, :].astype(y_ref.dtype)
    for _, kb_ref, vb_ref, _, ko_ref, vo_ref in views:
        wb = kb_ref.shape[0]
        ko_ref[0:wb - 1, :] = kb_ref[1:wb, :]
        vo_ref[0:wb - 1, :] = vb_ref[1:wb, :]


def _mix_sample_abc(z_ref, sa_ref, sp_ref, sc_ref, wa_ref, wp_ref, ps_ref, wc_ref, cb_ref, lg_ref, lb_ref,
                    y_ref, sao_ref, spo_ref, sco_ref, *, start_pos):
    gw = GROUP_W
    zc = lambda c, w=gw: z_ref[:, c:c + w]

    ua = zc(2 * gw) * zc(0)
    conv = wa_ref[0:1, :] * sa_ref[0:1, :] + wa_ref[1:2, :] * sa_ref[1:2, :] + wa_ref[2:3, :] * ua
    y_ref[:, 0:gw] = (zc(gw) * conv).astype(y_ref.dtype)
    sao_ref[0:1, :] = sa_ref[1:2, :]
    sao_ref[1:2, :] = ua

    ub = zc(3 * gw)
    for gi, win in enumerate(POOL_WINDOWS):
        lanes = slice(gi * POOL_GROUP_W, (gi + 1) * POOL_GROUP_W)
        u = ub[:, lanes]
        s = u + jnp.sum(sp_ref[POOL_STATE - (win - 1):POOL_STATE, lanes], axis=0, keepdims=True)
        cnt = float(min(start_pos + 1, win))
        d = s / cnt - u
        yb = jnp.dot(d.astype(BF16), wp_ref[gi], preferred_element_type=F32) * ps_ref[:, lanes]
        y_ref[:, gw + gi * POOL_GROUP_W:gw + (gi + 1) * POOL_GROUP_W] = yb.astype(y_ref.dtype)
    spo_ref[0:POOL_STATE - 1, :] = sp_ref[1:POOL_STATE, :]
    spo_ref[POOL_STATE - 1:POOL_STATE, :] = ub

    uc = zc(4 * gw) * jax.nn.sigmoid(zc(5 * gw))
    kc = CONV_C_W - 1
    acc = jnp.sum(wc_ref[0:kc, :] * sc_ref[...], axis=0, keepdims=True) + wc_ref[kc:kc + 1, :] * uc
    yc = _layer_norm_silu(acc + cb_ref[...], lg_ref[...], lb_ref[...])
    y_ref[:, 2 * gw:3 * gw] = yc.astype(y_ref.dtype)
    sco_ref[0:kc - 1, :] = sc_ref[1:kc, :]
    sco_ref[kc - 1:kc, :] = uc


def mix_sample(z, sa, sp, sc, kb, vb, start_pos, conv_a_w, pool_w, pool_scale, conv_c_w, conv_c_b, ln_g, ln_b,
               q_norm_g, k_norm_g, sinks, bias_row):
    nseq, in_cols = z.shape
    wb = kb.shape[1]
    spb = math.gcd(nseq, SAMPLE_SEQS)
    assert wb == WINDOW
    per_seq = lambda a: pl.BlockSpec((spb,) + a.shape[1:], lambda s: (s,) + (0,) * (a.ndim - 1))
    full = lambda a: pl.BlockSpec(a.shape, lambda s: (0,) * a.ndim)
    z3 = z.reshape(nseq, 1, in_cols)
    outs = pl.pallas_call(
        functools.partial(_mix_sample_kernel, start_pos=start_pos),
        out_shape=(
            jax.ShapeDtypeStruct((nseq, 1, 4 * GROUP_W), BF16),
            jax.ShapeDtypeStruct(sa.shape, F32), jax.ShapeDtypeStruct(sp.shape, F32),
            jax.ShapeDtypeStruct(sc.shape, F32), jax.ShapeDtypeStruct(kb.shape, F32),
            jax.ShapeDtypeStruct(vb.shape, F32),
        ),
        grid=(nseq // spb,),
        in_specs=[pl.BlockSpec(memory_space=pltpu.SMEM)] + [per_seq(a) for a in (z3, sa, sp, sc, kb, vb)] + [
            full(a) for a in (conv_a_w, pool_w, pool_scale, conv_c_w, conv_c_b, ln_g, ln_b,
                              q_norm_g, k_norm_g, bias_row)],
        out_specs=tuple(per_seq(a) for a in (jax.ShapeDtypeStruct((nseq, 1, 4 * GROUP_W), BF16), sa, sp, sc, kb, vb)),
        compiler_params=_params("parallel"),
        name="mix_sample",
    )(sinks, z3, sa, sp, sc, kb, vb, conv_a_w, pool_w, pool_scale, conv_c_w, conv_c_b, ln_g, ln_b,
      q_norm_g, k_norm_g, bias_row)
    return (outs[0].reshape(nseq, 4 * GROUP_W),) + tuple(outs[1:])


def kernel(x_prompt, x_sample, state_conv_a, state_pool, state_conv_c, cache_k_win, cache_v_win, rel_bias,
           norm_mix_g, w_in, conv_a_w, pool_w, pool_scale, conv_c_w, conv_c_b, ln_c_g, ln_c_b, q_norm_g, k_norm_g,
           attn_sinks, w_out, norm_ffn_g, w_gate, w_up, w_down):
    bsz, seq, d_model = x_prompt.shape
    nseq = x_sample.shape[0]
    depth = w_in.shape[0]
    wb = cache_k_win.shape[2]

    pool_w_b = pool_w.astype(BF16)
    bias = window_bias(rel_bias)
    bias_row = bias[0, np.argsort(_bias_slot_heads()), 0, :]
    kbuf = cache_k_win.reshape(depth, nseq, wb, KV_W)
    vbuf = cache_v_win.reshape(depth, nseq, wb, KV_W)
    row = lambda a, l: a[l][None, :]
    mixer_w = [(conv_a_w[l], pool_w_b[l], row(pool_scale, l), conv_c_w[l], row(conv_c_b, l),
                row(ln_c_g, l), row(ln_c_b, l)) for l in range(depth)]

    xs = x_sample.reshape(nseq, d_model)
    xp = x_prompt.reshape(bsz * seq, d_model)
    new_s = [[] for _ in range(5)]
    new_p = [[] for _ in range(5)]
    hb, ss = norm_prep(xp, row(norm_mix_g, 0), **TILES["norm_prep"])
    w_in_b = w_out_b = w_gate_b = w_up_b = w_down_b = None
    for l in range(depth):
        cast = l if l == 0 else None
        stacked = lambda w, wb_: w if l == 0 else wb_
        zs, *made = norm_matmul_small(xs, row(norm_mix_g, l), stacked(w_in, w_in_b), cast,
                                      **TILES["sample_in_proj"])
        w_in_b = made[0] if made else w_in_b
        y_s, sa_s, sp_s, sc_s, k_s, v_s = mix_sample(
            zs, state_conv_a[l], state_pool[l], state_conv_c[l], kbuf[l], vbuf[l], PAST_LEN,
            *mixer_w[l], row(q_norm_g, l), row(k_norm_g, l), attn_sinks[l], bias_row)
        xs, *made = resid_matmul_small(y_s, stacked(w_out, w_out_b), cast, xs, **TILES["sample_out_proj"])
        w_out_b = made[0] if made else w_out_b
        ffs, *made = norm_gateup_small(xs, row(norm_ffn_g, l), stacked(w_gate, w_gate_b), stacked(w_up, w_up_b),
                                       cast, **TILES["sample_gate_up"])
        w_gate_b, w_up_b = made if made else (w_gate_b, w_up_b)
        xs, *made = resid_matmul_small(ffs, stacked(w_down, w_down_b), cast, xs, **TILES["sample_down_proj"])
        w_down_b = made[0] if made else w_down_b
        for i, a in enumerate((sa_s, sp_s, sc_s, k_s, v_s)):
            new_s[i].append(a)

        more = l + 1 < depth
        side = lambda *ws: tuple((w, l + 1, swap) for w, swap in ws) if more else ()
        zp, *nxt_in = scaled_matmul(hb, ss, w_in_b, side((w_in, False)), **TILES["in_proj"])
        y_abc, sa_p, sp_p, sc_p = mix_abc_prompt(zp, bsz, seq, *mixer_w[l], **TILES["mix_abc"])
        y_d, k_p, v_p = attn_prompt(zp, bsz, seq, row(q_norm_g, l), row(k_norm_g, l), attn_sinks[l], bias)
        xp, hb, ss, *nxt_out = resid_matmul((y_abc, y_d), w_out_b, xp, row(norm_ffn_g, l), side((w_out, False)),
                                            **TILES["out_proj"])
        ff, *nxt_ffn = scaled_gateup(hb, ss, w_gate_b, w_up_b, side((w_gate, False), (w_up, False), (w_down, True)),
                                     **TILES["gate_up"])
        if more:
            xp, hb, ss = resid_matmul((ff,), w_down_b, xp, row(norm_mix_g, l + 1), **TILES["down_proj"])
            (w_in_b,), (w_out_b,), (w_gate_b, w_up_b, w_down_b) = nxt_in, nxt_out, nxt_ffn
        else:
            xp, = resid_matmul((ff,), w_down_b, xp, **TILES["down_proj"])
        for i, a in enumerate((sa_p, sp_p, sc_p, k_p, v_p)):
            new_p[i].append(a)

    heads = lambda a: a.reshape(a.shape[:-1] + (N_KV, HEAD_DIM))
    stacked_p = [jnp.stack(a) for a in new_p]
    stacked_s = [jnp.stack(a) for a in new_s]
    return (xp.reshape(bsz, seq, d_model), xs.reshape(nseq, 1, d_model),
            *stacked_p[:3], heads(stacked_p[3]), heads(stacked_p[4]),
            *stacked_s[:3], heads(stacked_s[3]), heads(stacked_s[4]))
```

```python
import functools
import math

import numpy as np
import jax
import jax.numpy as jnp
from jax import lax
from jax.experimental import pallas as pl
from jax.experimental.pallas import tpu as pltpu

F32 = jnp.float32
BF16 = jnp.bfloat16

GROUP_W = 1024
HEAD_DIM = 64
N_HEADS = GROUP_W // HEAD_DIM
N_KV = 4
Q_PER_KV = N_HEADS // N_KV
KV_W = N_KV * HEAD_DIM
WINDOW = 128
CONV_A_W = 3
CONV_C_W = 31
POOL_WINDOWS = (2, 4, 8, 16)
POOL_GROUP_W = GROUP_W // len(POOL_WINDOWS)
POOL_STATE = max(POOL_WINDOWS) - 1
NUM_BUCKETS = 32
MAX_DISTANCE = 128
EPS = 1e-6
NEG_INF = float("-inf")

PAST_LEN = 8192

LANES = 128
SUBLANES = 8
SAMPLE_SEQS = 4

TILES = {
    "norm_prep": dict(tm=512),
    "in_proj": dict(tm=1024, tn=768),
    "out_proj": dict(tm=1024, tn=512),
    "gate_up": dict(tm=2048, tn=256),
    "down_proj": dict(tm=512, tn=512),
    "sample_in_proj": dict(tn=512),
    "sample_out_proj": dict(tn=512),
    "sample_gate_up": dict(tn=256),
    "sample_down_proj": dict(tn=256),
    "sample_in_proj_bf16": dict(tn=1536),
    "sample_out_proj_bf16": dict(tn=1024),
    "sample_gate_up_bf16": dict(tn=256),
    "sample_down_proj_bf16": dict(tn=512),
    "mix_abc": dict(tb=256),
}
MIX_CHUNK = 128
MXU_COLS = 256
HALO = 32
NORM_ROWS = 16
VMEM_LIMIT = 56 << 20


def _params(*sem):
    return pltpu.CompilerParams(dimension_semantics=sem, vmem_limit_bytes=VMEM_LIMIT)


def _rmsnorm_to(x_ref, g_ref, h_ref):
    rows = x_ref.shape[0]
    step = min(NORM_ROWS, rows)

    def body(i, carry):
        r = pl.multiple_of(i * step, step)
        x = x_ref[pl.ds(r, step), :]
        ms = jnp.mean(x * x, axis=-1, keepdims=True)
        h_ref[pl.ds(r, step), :] = (x * lax.rsqrt(ms + EPS) * g_ref[...]).astype(h_ref.dtype)
        return carry

    lax.fori_loop(0, rows // step, body, 0)


def _sample_w_spec(w, layer, tn):
    if layer is None:
        return pl.BlockSpec((w.shape[0], tn), lambda j: (0, j))
    return pl.BlockSpec((None, w.shape[1], tn), lambda j: (layer, 0, j))


def _bf16_tile(w_ref, wb_ref):
    if wb_ref is None:
        return w_ref[...]
    wb_ref[...] = w_ref[...].astype(BF16)
    return wb_ref[...]


def _norm_mm_small_kernel(x_ref, g_ref, w_ref, o_ref, *rest):
    wb_ref, h_ref = rest if len(rest) == 2 else (None, rest[0])

    @pl.when(pl.program_id(0) == 0)
    def _():
        _rmsnorm_to(x_ref, g_ref, h_ref)

    o_ref[...] = jnp.dot(h_ref[...], _bf16_tile(w_ref, wb_ref), preferred_element_type=F32)


def norm_matmul_small(x, g, w, layer, *, tn):
    m, d = x.shape
    n = w.shape[-1]
    assert n % tn == 0
    out_shape = [jax.ShapeDtypeStruct((m, n), F32)]
    out_specs = [pl.BlockSpec((m, tn), lambda j: (0, j))]
    if layer is not None:
        out_shape.append(jax.ShapeDtypeStruct((d, n), BF16))
        out_specs.append(pl.BlockSpec((d, tn), lambda j: (0, j)))
    return pl.pallas_call(
        _norm_mm_small_kernel,
        out_shape=tuple(out_shape),
        grid=(n // tn,),
        in_specs=[pl.BlockSpec((m, d), lambda j: (0, 0)), pl.BlockSpec((1, d), lambda j: (0, 0)),
                  _sample_w_spec(w, layer, tn)],
        out_specs=tuple(out_specs),
        scratch_shapes=[pltpu.VMEM((m, d), BF16)],
        compiler_params=_params("arbitrary"),
        name="sample_in_proj",
    )(x, g, w)


def _norm_gateup_small_kernel(x_ref, g_ref, wg_ref, wu_ref, o_ref, *rest):
    wgb_ref, wub_ref, h_ref = rest if len(rest) == 3 else (None, None, rest[0])

    @pl.when(pl.program_id(0) == 0)
    def _():
        _rmsnorm_to(x_ref, g_ref, h_ref)

    h = h_ref[...]
    a = jnp.dot(h, _bf16_tile(wg_ref, wgb_ref), preferred_element_type=F32)
    b = jnp.dot(h, _bf16_tile(wu_ref, wub_ref), preferred_element_type=F32)
    o_ref[...] = (a * jax.nn.sigmoid(a) * b).astype(o_ref.dtype)


def norm_gateup_small(x, g, w_gate, w_up, layer, *, tn):
    m, d = x.shape
    f = w_gate.shape[-1]
    assert f % tn == 0
    out_shape = [jax.ShapeDtypeStruct((m, f), BF16)]
    out_specs = [pl.BlockSpec((m, tn), lambda j: (0, j))]
    if layer is not None:
        out_shape += [jax.ShapeDtypeStruct((d, f), BF16)] * 2
        out_specs += [pl.BlockSpec((d, tn), lambda j: (0, j))] * 2
    return pl.pallas_call(
        _norm_gateup_small_kernel,
        out_shape=tuple(out_shape),
        grid=(f // tn,),
        in_specs=[pl.BlockSpec((m, d), lambda j: (0, 0)), pl.BlockSpec((1, d), lambda j: (0, 0)),
                  _sample_w_spec(w_gate, layer, tn), _sample_w_spec(w_up, layer, tn)],
        out_specs=tuple(out_specs),
        scratch_shapes=[pltpu.VMEM((m, d), BF16)],
        compiler_params=_params("arbitrary"),
        name="sample_gate_up",
    )(x, g, w_gate, w_up)


def _resid_mm_small_kernel(a_ref, w_ref, r_ref, o_ref, wb_ref=None):
    o_ref[...] = r_ref[...] + jnp.dot(a_ref[...], _bf16_tile(w_ref, wb_ref), preferred_element_type=F32)


def resid_matmul_small(a, w, layer, resid, *, tn):
    m, n = resid.shape
    k = a.shape[1]
    assert w.shape[-2] == k and n % tn == 0
    out_shape = [jax.ShapeDtypeStruct((m, n), F32)]
    out_specs = [pl.BlockSpec((m, tn), lambda j: (0, j))]
    if layer is not None:
        out_shape.append(jax.ShapeDtypeStruct((k, n), BF16))
        out_specs.append(pl.BlockSpec((k, tn), lambda j: (0, j)))
    return pl.pallas_call(
        _resid_mm_small_kernel,
        out_shape=tuple(out_shape),
        grid=(n // tn,),
        in_specs=[pl.BlockSpec((m, k), lambda j: (0, 0)), _sample_w_spec(w, layer, tn),
                  pl.BlockSpec((m, tn), lambda j: (0, j))],
        out_specs=tuple(out_specs),
        compiler_params=_params("arbitrary"),
        name="sample_proj_residual",
    )(a, w, resid)


def _sum_lane_groups(x):
    acc = x[:, 0:LANES]
    for k in range(1, x.shape[1] // LANES):
        acc = acc + x[:, k * LANES:(k + 1) * LANES]
    return acc


def _rowscale_to(ss_ref, rs_ref, d):
    tot = jnp.sum(_sum_lane_groups(ss_ref[...]), axis=-1, keepdims=True)
    rs_ref[...] = jnp.broadcast_to(lax.rsqrt(tot / d + EPS), rs_ref.shape)


def _norm_prep_kernel(x_ref, g_ref, hb_ref, ss_ref):
    rows = x_ref.shape[0]

    def body(i, carry):
        r = pl.multiple_of(i * NORM_ROWS, NORM_ROWS)
        x = x_ref[pl.ds(r, NORM_ROWS), :]
        hb_ref[pl.ds(r, NORM_ROWS), :] = (x * g_ref[...]).astype(BF16)
        ss_ref[pl.ds(r, NORM_ROWS), :] = _sum_lane_groups(x * x)
        return carry

    lax.fori_loop(0, rows // NORM_ROWS, body, 0)


def norm_prep(x, g, *, tm):
    m, d = x.shape
    assert m % tm == 0 and tm % NORM_ROWS == 0
    return pl.pallas_call(
        _norm_prep_kernel,
        out_shape=(jax.ShapeDtypeStruct((m, d), BF16), jax.ShapeDtypeStruct((m, LANES), F32)),
        grid=(m // tm,),
        in_specs=[pl.BlockSpec((tm, d), lambda i: (i, 0)), pl.BlockSpec((1, d), lambda i: (0, 0))],
        out_specs=(pl.BlockSpec((tm, d), lambda i: (i, 0)), pl.BlockSpec((tm, LANES), lambda i: (i, 0))),
        compiler_params=_params("parallel"),
        name="norm_prep",
    )(x, g)


def _side_cast_specs(side_casts, gm, gn):
    in_specs, out_specs, out_shape, args = [], [], [], []
    for w, layer, swap in side_casts:
        _, r, c = w.shape
        gr, gc = (gn, gm) if swap else (gm, gn)
        assert r % (gr * 2 * SUBLANES) == 0 and c % (gc * LANES) == 0
        block = (r // gr, c // gc)
        pick = (lambda i, j: (j, i)) if swap else (lambda i, j: (i, j))
        in_specs.append(pl.BlockSpec((None,) + block, lambda i, j, pick=pick, layer=layer: (layer,) + pick(i, j)))
        out_specs.append(pl.BlockSpec(block, pick))
        out_shape.append(jax.ShapeDtypeStruct((r, c), BF16))
        args.append(w)
    return in_specs, out_specs, out_shape, args


def _run_side_casts(src_refs, dst_refs):
    for src, dst in zip(src_refs, dst_refs):
        dst[...] = src[...].astype(BF16)


def _scaled_mm_kernel(hb_ref, ss_ref, w_ref, *refs):
    n_side = (len(refs) - 2) // 2
    o_ref, rs_ref = refs[n_side], refs[-1]

    @pl.when(pl.program_id(1) == 0)
    def _():
        _rowscale_to(ss_ref, rs_ref, hb_ref.shape[1])

    _run_side_casts(refs[:n_side], refs[n_side + 1:-1])
    acc = jnp.dot(hb_ref[...], w_ref[...], preferred_element_type=F32)
    rs = rs_ref[...]
    for k in range(o_ref.shape[1] // LANES):
        o_ref[:, k * LANES:(k + 1) * LANES] = acc[:, k * LANES:(k + 1) * LANES] * rs


def scaled_matmul(hb, ss, w, side_casts=(), *, tm, tn):
    m, d = hb.shape
    n = w.shape[1]
    assert m % tm == 0 and n % tn == 0 and tn % LANES == 0
    s_in, s_out, s_shape, s_args = _side_cast_specs(side_casts, m // tm, n // tn)
    return pl.pallas_call(
        _scaled_mm_kernel,
        out_shape=(jax.ShapeDtypeStruct((m, n), F32), *s_shape),
        grid=(m // tm, n // tn),
        in_specs=[
            pl.BlockSpec((tm, d), lambda i, j: (i, 0)),
            pl.BlockSpec((tm, ss.shape[1]), lambda i, j: (i, 0)),
            pl.BlockSpec((d, tn), lambda i, j: (0, j)),
            *s_in,
        ],
        out_specs=(pl.BlockSpec((tm, tn), lambda i, j: (i, j)), *s_out),
        scratch_shapes=[pltpu.VMEM((tm, LANES), F32)],
        compiler_params=_params("parallel", "arbitrary"),
        name="in_proj",
    )(hb, ss, w, *s_args)


def _scaled_gateup_kernel(hb_ref, ss_ref, wg_ref, wu_ref, *refs):
    n_side = (len(refs) - 2) // 2
    o_ref, rs_ref = refs[n_side], refs[-1]

    @pl.when(pl.program_id(1) == 0)
    def _():
        _rowscale_to(ss_ref, rs_ref, hb_ref.shape[1])

    _run_side_casts(refs[:n_side], refs[n_side + 1:-1])
    half = hb_ref.shape[0] // 2
    for r0 in (0, half):
        rows = slice(r0, r0 + half)
        h = hb_ref[rows, :]
        a = jnp.dot(h, wg_ref[...], preferred_element_type=F32)
        b = jnp.dot(h, wu_ref[...], preferred_element_type=F32)
        rs = rs_ref[rows, :]
        for k in range(o_ref.shape[1] // LANES):
            cols = slice(k * LANES, (k + 1) * LANES)
            ak = a[:, cols] * rs
            o_ref[rows, cols] = (ak * jax.nn.sigmoid(ak) * (b[:, cols] * rs)).astype(o_ref.dtype)


def scaled_gateup(hb, ss, w_gate, w_up, side_casts=(), *, tm, tn):
    m, d = hb.shape
    f = w_gate.shape[1]
    assert m % tm == 0 and f % tn == 0 and tn % LANES == 0
    w_spec = pl.BlockSpec((d, tn), lambda i, j: (0, j))
    s_in, s_out, s_shape, s_args = _side_cast_specs(side_casts, m // tm, f // tn)
    return pl.pallas_call(
        _scaled_gateup_kernel,
        out_shape=(jax.ShapeDtypeStruct((m, f), BF16), *s_shape),
        grid=(m // tm, f // tn),
        in_specs=[
            pl.BlockSpec((tm, d), lambda i, j: (i, 0), pipeline_mode=pl.Buffered(1)),
            pl.BlockSpec((tm, ss.shape[1]), lambda i, j: (i, 0)),
            w_spec, w_spec, *s_in],
        out_specs=(pl.BlockSpec((tm, tn), lambda i, j: (i, j)), *s_out),
        scratch_shapes=[pltpu.VMEM((tm, LANES), F32)],
        compiler_params=_params("parallel", "arbitrary"),
        name="gate_up",
    )(hb, ss, w_gate, w_up, *s_args)


def _resid_mm_kernel(*refs, k_splits, emit_norm, n_side):
    n_a = len(k_splits)
    a_refs = refs[:n_a]
    n_in = 3 if emit_norm else 2
    n_out = 3 if emit_norm else 1
    main_in = refs[n_a:n_a + n_in]
    side_in = refs[n_a + n_in:n_a + n_in + n_side]
    main_out = refs[n_a + n_in + n_side:n_a + n_in + n_side + n_out]
    side_out = refs[n_a + n_in + n_side + n_out:]
    if emit_norm:
        w_ref, r_ref, g_ref = main_in
        o_ref, hb_ref, ss_ref = main_out
    else:
        w_ref, r_ref = main_in
        o_ref, = main_out
    _run_side_casts(side_in, side_out)
    if emit_norm:
        @pl.when(pl.program_id(1) == 0)
        def _():
            ss_ref[...] = jnp.zeros_like(ss_ref)

    tn = o_ref.shape[1]
    part = None
    for c in range(0, tn, MXU_COLS):
        cols = slice(c, c + MXU_COLS)
        acc = r_ref[:, cols]
        off = 0
        for a_ref, k in zip(a_refs, k_splits):
            acc = acc + jnp.dot(a_ref[...], w_ref[off:off + k, cols], preferred_element_type=F32)
            off += k
        o_ref[:, cols] = acc
        if emit_norm:
            hb_ref[:, cols] = (acc * g_ref[:, cols]).astype(BF16)
            ss = _sum_lane_groups(acc * acc)
            part = ss if part is None else part + ss
    if emit_norm:
        ss_ref[...] += part


def resid_matmul(acts, w, resid, g_next=None, side_casts=(), *, tm, tn):
    m, n = resid.shape
    k_splits = tuple(a.shape[1] for a in acts)
    k_total = sum(k_splits)
    emit_norm = g_next is not None
    assert w.shape[0] == k_total and m % tm == 0 and n % tn == 0 and tn % MXU_COLS == 0
    tile = pl.BlockSpec((tm, tn), lambda i, j: (i, j))
    in_specs = [pl.BlockSpec((tm, k), lambda i, j: (i, 0)) for k in k_splits] + [
        pl.BlockSpec((k_total, tn), lambda i, j: (0, j)), tile]
    out_shape = [jax.ShapeDtypeStruct((m, n), F32)]
    out_specs = [tile]
    args = (*acts, w, resid)
    if emit_norm:
        in_specs.append(pl.BlockSpec((1, tn), lambda i, j: (0, j)))
        out_shape += [jax.ShapeDtypeStruct((m, n), BF16), jax.ShapeDtypeStruct((m, LANES), F32)]
        out_specs += [tile, pl.BlockSpec((tm, LANES), lambda i, j: (i, 0))]
        args = args + (g_next,)
    s_in, s_out, s_shape, s_args = _side_cast_specs(side_casts, m // tm, n // tn)
    return pl.pallas_call(
        functools.partial(_resid_mm_kernel, k_splits=k_splits, emit_norm=emit_norm, n_side=len(s_args)),
        out_shape=(*out_shape, *s_shape),
        grid=(m // tm, n // tn),
        in_specs=in_specs + s_in,
        out_specs=(*out_specs, *s_out),
        compiler_params=_params("parallel", "arbitrary"),
        name="proj_residual",
    )(*args, *s_args)


def _bucket_matrix():
    i = np.arange(WINDOW)[:, None]
    j = np.arange(2 * WINDOW)[None, :]
    rel = i + WINDOW - j
    max_exact = NUM_BUCKETS // 2
    nf = np.maximum(rel, 1).astype(np.float32)
    large = max_exact + (np.log(nf / max_exact) / math.log(MAX_DISTANCE / max_exact)
                         * (NUM_BUCKETS - max_exact)).astype(np.int32)
    large = np.minimum(large, NUM_BUCKETS - 1)
    bucket = np.where(rel < max_exact, rel, large)
    valid = (rel >= 0) & (rel < WINDOW)
    later = np.where(valid, bucket, -1).astype(np.int32)
    first = np.where(j >= WINDOW, later, -1)
    return np.stack([later, first])


def _bias_kernel(rb_ref, bk_ref, o_ref):
    t = pl.program_id(1)
    per = N_HEADS // N_KV
    h = per * (t // per) + 2 * (t % 2) + (t % per) // 2
    bk = bk_ref[...]
    acc = jnp.full(bk.shape, NEG_INF, F32)
    for b in range(NUM_BUCKETS):
        acc = jnp.where(bk == b, rb_ref[b, h], acc)
    o_ref[...] = acc


def window_bias(rel_bias):
    bucket = jnp.asarray(_bucket_matrix())
    return pl.pallas_call(
        _bias_kernel,
        out_shape=jax.ShapeDtypeStruct((2, N_HEADS, WINDOW, 2 * WINDOW), F32),
        grid=(2, N_HEADS),
        in_specs=[
            pl.BlockSpec(memory_space=pltpu.SMEM),
            pl.BlockSpec((None, WINDOW, 2 * WINDOW), lambda f, h: (f, 0, 0)),
        ],
        out_specs=pl.BlockSpec((None, None, WINDOW, 2 * WINDOW), lambda f, h: (f, h, 0, 0)),
        compiler_params=_params("arbitrary", "arbitrary"),
        name="window_bias",
    )(rel_bias, bucket)


def _layer_norm_silu(y, g, b):
    mu = jnp.mean(y, axis=-1, keepdims=True)
    var = jnp.mean(jnp.square(y - mu), axis=-1, keepdims=True)
    y = (y - mu) * lax.rsqrt(var + EPS) * g + b
    return y * jax.nn.sigmoid(y)


def _delay_rows(blocks, b):
    if b == 0:
        return blocks[1:]
    return [jnp.concatenate([blocks[i], blocks[i + 1]], axis=0)[SUBLANES - b:2 * SUBLANES - b]
            for i in range(len(blocks) - 1)]


def _trailing_sum(blocks, n):
    step = 1
    while step < n:
        blocks = [x + y for x, y in zip(blocks[1:], _delay_rows(blocks, step))]
        step *= 2
    return blocks


def _store_bf16_rows(ref, r0, cols, blocks):
    for p in range(len(blocks) // 2):
        pair = jnp.concatenate([blocks[2 * p], blocks[2 * p + 1]], axis=0)
        ref[pl.ds(r0 + 2 * SUBLANES * p, 2 * SUBLANES), cols] = pair.astype(ref.dtype)


def _mix_abc_kernel(ha_ref, ba_ref, ca_ref, ub_ref, ac_ref, gc_ref,
                    wa_ref, wp_ref, ps_ref, wc_ref, cb_ref, lg_ref, lb_ref,
                    y_ref, sa_ref, sp_ref, sc_ref,
                    ea_ref, eb_ref, ec_ref, d_ref, pre_ref, *, tb):
    t = pl.program_id(1)
    nt = pl.num_programs(1)
    nblk = MIX_CHUNK // SUBLANES
    hist = HALO // SUBLANES

    @pl.when(t == 0)
    def _():
        zeros = jnp.zeros((HALO, GROUP_W), F32)
        ea_ref[0:HALO, :] = zeros
        eb_ref[0:HALO, :] = zeros
        ec_ref[0:HALO, :] = zeros

    ea_ref[HALO:HALO + tb, :] = ca_ref[...] * ha_ref[...]
    eb_ref[HALO:HALO + tb, :] = ub_ref[...]
    ec_ref[HALO:HALO + tb, :] = ac_ref[...] * jax.nn.sigmoid(gc_ref[...])

    def chunk(ci, carry):
        r0 = pl.multiple_of(ci * MIX_CHUNK, MIX_CHUNK)
        load = lambda ref, j, cols: ref[pl.ds(r0 + SUBLANES * j, SUBLANES), cols]
        row_pos = t * tb + r0 + lax.broadcasted_iota(jnp.int32, (SUBLANES, LANES), 0)
        for c in range(GROUP_W // LANES):
            cols = slice(c * LANES, (c + 1) * LANES)

            xa = [load(ea_ref, j, cols) for j in range(hist - 1, hist + nblk)]
            conv = [wa_ref[CONV_A_W - 1:CONV_A_W, cols] * x for x in xa[1:]]
            for b in range(1, CONV_A_W):
                w = wa_ref[CONV_A_W - 1 - b:CONV_A_W - b, cols]
                conv = [x + y for x, y in zip(conv, _delay_rows([w * x for x in xa], b))]
            ya = [ba_ref[pl.ds(r0 + SUBLANES * i, SUBLANES), cols] * conv[i] for i in range(nblk)]
            _store_bf16_rows(y_ref, r0, cols, ya)

            win = POOL_WINDOWS[c * LANES // POOL_GROUP_W]
            xb = [load(eb_ref, j, cols) for j in range(hist + nblk)]
            if win > SUBLANES:
                assert win == 2 * SUBLANES
                sums = _trailing_sum([x + y for x, y in zip(xb[1:], xb[:-1])], SUBLANES)
            else:
                sums = _trailing_sum(xb, win)
            sums = sums[len(sums) - nblk:]
            db = []
            for i in range(nblk):
                cnt = jnp.minimum(row_pos + (SUBLANES * i + 1), win).astype(F32)
                db.append(sums[i] / cnt - xb[hist + i])
            _store_bf16_rows(d_ref, r0, cols, db)

            xc = [load(ec_ref, j, cols) for j in range(hist + nblk)]
            acc = None
            for b in range(SUBLANES):
                q = None
                for a in range(hist):
                    delay = SUBLANES * a + b
                    if delay > CONV_C_W - 1:
                        continue
                    w = wc_ref[CONV_C_W - 1 - delay:CONV_C_W - delay, cols]
                    terms = [w * xc[hist - 1 - a + i] for i in range(nblk + 1)]
                    q = terms if q is None else [x + y for x, y in zip(q, terms)]
                q = _delay_rows(q, b)
                acc = q if acc is None else [x + y for x, y in zip(acc, q)]
            for i in range(nblk):
                pre_ref[pl.ds(r0 + SUBLANES * i, SUBLANES), cols] = acc[i] + cb_ref[:, cols]
        return carry

    lax.fori_loop(0, tb // MIX_CHUNK, chunk, 0)

    for gi in range(len(POOL_WINDOWS)):
        lanes = slice(gi * POOL_GROUP_W, (gi + 1) * POOL_GROUP_W)
        yb = jnp.dot(d_ref[:, lanes], wp_ref[gi], preferred_element_type=F32) * ps_ref[:, lanes]
        y_ref[:, GROUP_W + gi * POOL_GROUP_W:GROUP_W + (gi + 1) * POOL_GROUP_W] = yb.astype(y_ref.dtype)

    yc = _layer_norm_silu(pre_ref[...], lg_ref[...], lb_ref[...])
    y_ref[:, 2 * GROUP_W:3 * GROUP_W] = yc.astype(y_ref.dtype)

    @pl.when(t == nt - 1)
    def _():
        end = HALO + tb
        sa_ref[...] = ea_ref[end - (CONV_A_W - 1):end, :]
        sp_ref[...] = eb_ref[end - POOL_STATE:end, :]
        sc_ref[...] = ec_ref[end - (CONV_C_W - 1):end, :]

    ea_ref[0:HALO, :] = ea_ref[tb:tb + HALO, :]
    eb_ref[0:HALO, :] = eb_ref[tb:tb + HALO, :]
    ec_ref[0:HALO, :] = ec_ref[tb:tb + HALO, :]


def mix_abc_prompt(z, bsz, seq, conv_a_w, pool_w, pool_scale, conv_c_w, conv_c_b, ln_g, ln_b, *, tb):
    assert seq % tb == 0 and tb >= HALO and tb % MIX_CHUNK == 0
    assert HALO % SUBLANES == 0 and HALO >= CONV_C_W - 1 and HALO >= 2 * SUBLANES >= max(POOL_WINDOWS)
    nt = seq // tb
    col = lambda c: pl.BlockSpec((tb, GROUP_W), lambda b, t: (b * nt + t, c))
    full = lambda a: pl.BlockSpec(a.shape, lambda b, t: (0,) * a.ndim)
    state = lambda rows: pl.BlockSpec((None, rows, GROUP_W), lambda b, t: (b, 0, 0))
    ext = pltpu.VMEM((HALO + tb, GROUP_W), F32)
    return pl.pallas_call(
        functools.partial(_mix_abc_kernel, tb=tb),
        out_shape=(
            jax.ShapeDtypeStruct((bsz * seq, 3 * GROUP_W), BF16),
            jax.ShapeDtypeStruct((bsz, CONV_A_W - 1, GROUP_W), F32),
            jax.ShapeDtypeStruct((bsz, POOL_STATE, GROUP_W), F32),
            jax.ShapeDtypeStruct((bsz, CONV_C_W - 1, GROUP_W), F32),
        ),
        grid=(bsz, nt),
        in_specs=[col(c) for c in range(6)] + [
            full(conv_a_w), full(pool_w), full(pool_scale), full(conv_c_w),
            full(conv_c_b), full(ln_g), full(ln_b)],
        out_specs=(
            pl.BlockSpec((tb, 3 * GROUP_W), lambda b, t: (b * nt + t, 0)),
            state(CONV_A_W - 1), state(POOL_STATE), state(CONV_C_W - 1),
        ),
        scratch_shapes=[ext, ext, ext, pltpu.VMEM((tb, GROUP_W), BF16), pltpu.VMEM((tb, GROUP_W), F32)],
        compiler_params=_params("parallel", "arbitrary"),
        name="mix_abc_prompt",
    )(z, z, z, z, z, z, conv_a_w, pool_w, pool_scale, conv_c_w, conv_c_b, ln_g, ln_b)


def _head_rmsnorm(x, g, scale=1.0):
    ms = jnp.mean(x * x, axis=-1, keepdims=True)
    return x * (lax.rsqrt(ms + EPS) * scale) * g


def _headwise_rmsnorm(x, e, g):
    x2 = x * x
    hi = x2.astype(BF16)
    lo = (x2 - hi.astype(F32)).astype(BF16)
    sums = []
    for c in range(0, x.shape[1], MXU_COLS):
        cols = slice(c, c + MXU_COLS)
        sums.append(jnp.dot(hi[:, cols], e, preferred_element_type=F32)
                    + jnp.dot(lo[:, cols], e, preferred_element_type=F32))
    ss = jnp.concatenate(sums, axis=1)
    return x * lax.rsqrt(ss * (1.0 / HEAD_DIM) + EPS) * g


def _attn_prompt_kernel(sink_ref, q_ref, kv_ref, qg_ref, kg_ref, e_ref, bias_ref,
                        y_ref, ko_ref, vo_ref, kp_ref, vp_ref, s_ref, rhs_ref, p_ref, es_ref):
    n = pl.program_id(1)

    @pl.when(n == 0)
    def _():
        kp_ref[...] = jnp.zeros_like(kp_ref)
        vp_ref[...] = jnp.zeros_like(vp_ref)

    first = (n == 0).astype(jnp.int32)
    e = e_ref[...]
    qn = _headwise_rmsnorm(q_ref[...], e, qg_ref[...]).astype(BF16)
    kn = _headwise_rmsnorm(kv_ref[:, 0:KV_W], e, kg_ref[...])
    v = kv_ref[:, KV_W:2 * KV_W]
    ko_ref[...] = kn
    vo_ref[...] = v
    knb = kn.astype(BF16)
    vb = v.astype(BF16)

    low_q = lax.broadcasted_iota(jnp.int32, (2 * WINDOW, LANES), 1) < HEAD_DIM
    ones_half = (jnp.where(low_q, 1.0, 0.0).astype(BF16), jnp.where(low_q, 0.0, 1.0).astype(BF16))
    keep_lo, keep_hi = ones_half
    tiles = [(g, half) for g in range(N_KV) for half in range(2)]

    for i, (g, half) in enumerate(tiles):
        if half == 0:
            lanes = slice((g // 2) * LANES, (g // 2 + 1) * LANES)
            k_slab = jnp.concatenate([kp_ref[:, lanes], knb[:, lanes]], axis=0)
            v_slab = jnp.concatenate([vp_ref[:, lanes], vb[:, lanes]], axis=0)
            k_swap = pltpu.roll(k_slab, HEAD_DIM, axis=1)
            v_swap = pltpu.roll(v_slab, HEAD_DIM, axis=1)
            if g % 2 == 0:
                k_half = (k_slab * keep_lo, k_swap * keep_hi)
                v_half = (v_slab * keep_lo, v_swap * keep_hi)
            else:
                k_half = (k_swap * keep_lo, k_slab * keep_hi)
                v_half = (v_swap * keep_lo, v_slab * keep_hi)
            s0 = 2 * g
            q2 = jnp.concatenate([qn[:, s0 * LANES:(s0 + 1) * LANES], qn[:, (s0 + 1) * LANES:(s0 + 2) * LANES]],
                                 axis=0)
        s_ref[i] = lax.dot_general(q2, k_half[half], (((1,), (1,)), ((), ())), preferred_element_type=F32)
        rhs_ref[i] = jnp.concatenate([v_half[half], ones_half[half]], axis=1)

    for i, (g, half) in enumerate(tiles):
        heads = (4 * g + half, 4 * g + 2 + half)
        t = N_HEADS // N_KV * g + 2 * half
        s = s_ref[i] + bias_ref[first, t:t + 2].reshape(2 * WINDOW, 2 * WINDOW)
        sink = jnp.concatenate([jnp.full((WINDOW, LANES), sink_ref[h], F32) for h in heads], axis=0)
        m = jnp.maximum(jnp.broadcast_to(jnp.max(s, axis=-1, keepdims=True), (2 * WINDOW, LANES)), sink)
        p_ref[i] = jnp.exp(s - jnp.concatenate([m, m], axis=1)).astype(BF16)
        es_ref[i] = jnp.exp(sink - m)

    for g in range(N_KV):
        lo, hi = 2 * g, 2 * g + 1
        acc = (jnp.dot(p_ref[lo], rhs_ref[lo], preferred_element_type=F32)
               + jnp.dot(p_ref[hi], rhs_ref[hi], preferred_element_type=F32))
        denom = acc[:, LANES:] + jnp.where(low_q, es_ref[lo], es_ref[hi])
        out = (acc[:, 0:LANES] / denom).astype(y_ref.dtype)
        y_ref[:, lo * LANES:(lo + 1) * LANES] = out[0:WINDOW]
        y_ref[:, hi * LANES:(hi + 1) * LANES] = out[WINDOW:2 * WINDOW]

    kp_ref[...] = knb
    vp_ref[...] = vb


def _bias_slot_heads():
    per = N_HEADS // N_KV
    return [per * (t // per) + 2 * (t % 2) + (t % per) // 2 for t in range(N_HEADS)]


def attn_prompt(z, bsz, seq, q_norm_g, k_norm_g, sinks, bias):
    assert seq % WINDOW == 0
    nb = seq // WINDOW
    q_col = 6 * GROUP_W // GROUP_W
    kv_col = 7 * GROUP_W // (2 * KV_W)
    qg = jnp.tile(q_norm_g, (1, N_HEADS)) * HEAD_DIM ** -0.5
    kg = jnp.tile(k_norm_g, (1, N_KV))
    e = jnp.asarray(np.kron(np.eye(MXU_COLS // HEAD_DIM), np.ones((HEAD_DIM, HEAD_DIM))), BF16)
    win = pl.BlockSpec((None, WINDOW, KV_W), lambda b, n: (b, 0, 0))
    full = lambda a: pl.BlockSpec(a.shape, lambda b, n: (0,) * a.ndim)
    return pl.pallas_call(
        _attn_prompt_kernel,
        out_shape=(
            jax.ShapeDtypeStruct((bsz * seq, GROUP_W), BF16),
            jax.ShapeDtypeStruct((bsz, WINDOW, KV_W), F32),
            jax.ShapeDtypeStruct((bsz, WINDOW, KV_W), F32),
        ),
        grid=(bsz, nb),
        in_specs=[
            pl.BlockSpec(memory_space=pltpu.SMEM),
            pl.BlockSpec((WINDOW, GROUP_W), lambda b, n: (b * nb + n, q_col)),
            pl.BlockSpec((WINDOW, 2 * KV_W), lambda b, n: (b * nb + n, kv_col)),
            full(qg), full(kg), full(e), full(bias),
        ],
        out_specs=(pl.BlockSpec((WINDOW, GROUP_W), lambda b, n: (b * nb + n, 0)), win, win),
        scratch_shapes=[
            pltpu.VMEM((WINDOW, KV_W), BF16), pltpu.VMEM((WINDOW, KV_W), BF16),
            pltpu.VMEM((2 * N_KV, 2 * WINDOW, 2 * WINDOW), F32),
            pltpu.VMEM((2 * N_KV, 2 * WINDOW, 2 * LANES), BF16),
            pltpu.VMEM((2 * N_KV, 2 * WINDOW, 2 * WINDOW), BF16),
            pltpu.VMEM((2 * N_KV, 2 * WINDOW, LANES), F32),
        ],
        compiler_params=_params("parallel", "arbitrary"),
        name="attn_prompt",
    )(sinks, z, z, qg, kg, e, bias)


def _mix_sample_kernel(sink_ref, *refs, start_pos):
    seq_in, shared, seq_out = refs[:6], refs[6:16], refs[16:]
    nseq = seq_in[0].shape[0]
    views = [tuple(r.at[s] for r in seq_in) + tuple(r.at[s] for r in seq_out) for s in range(nseq)]
    for z_ref, sa_ref, sp_ref, sc_ref, _, _, y_ref, sao_ref, spo_ref, sco_ref, _, _ in views:
        _mix_sample_abc(z_ref, sa_ref, sp_ref, sc_ref, *shared[:7], y_ref, sao_ref, spo_ref, sco_ref,
                        start_pos=start_pos)
    _mix_sample_attn(sink_ref, [(v[0], v[4], v[5], v[6], v[10], v[11]) for v in views], *shared[7:])


def _mix_sample_attn(sink_ref, views, qg_ref, kg_ref, bias_ref):
    gw = GROUP_W
    items = [(v, g) for v in views for g in range(N_KV)]
    scored = []
    for (z_ref, kb_ref, vb_ref, y_ref, ko_ref, vo_ref), g in items:
        wb = kb_ref.shape[0]
        zc = lambda c, w: z_ref[:, c:c + w]
        lanes = slice(g * HEAD_DIM, (g + 1) * HEAD_DIM)
        kn = _head_rmsnorm(zc(7 * gw + g * HEAD_DIM, HEAD_DIM), kg_ref[...])
        vn = zc(7 * gw + KV_W + g * HEAD_DIM, HEAD_DIM)
        ko_ref[wb - 1:wb, lanes] = kn
        vo_ref[wb - 1:wb, lanes] = vn
        qs, sinks = [], []
        for j in range(Q_PER_KV):
            h = g * Q_PER_KV + j
            qs.append(_head_rmsnorm(zc(6 * gw + h * HEAD_DIM, HEAD_DIM), qg_ref[...], HEAD_DIM ** -0.5))
            sinks.append(jnp.full((1, 1), sink_ref[h], F32))
        qcat = jnp.concatenate(qs, axis=0)
        sink = jnp.concatenate(sinks, axis=0)
        hb = bias_ref[g * Q_PER_KV:(g + 1) * Q_PER_KV, :]
        s_buf = lax.dot_general(qcat.astype(BF16), kb_ref[:, lanes].astype(BF16), (((1,), (1,)), ((), ())),
                                preferred_element_type=F32) + hb[:, WINDOW - wb:WINDOW]
        s_new = jnp.sum(qcat * kn, axis=-1, keepdims=True) + hb[:, WINDOW:WINDOW + 1]
        scored.append((s_buf, s_new, sink, vn))
    weighted = []
    for s_buf, s_new, sink, vn in scored:
        m = jnp.maximum(jnp.maximum(jnp.max(s_buf, axis=-1, keepdims=True), s_new), sink)
        p_buf = jnp.exp(s_buf - m)
        p_new = jnp.exp(s_new - m)
        denom = jnp.sum(p_buf, axis=-1, keepdims=True) + p_new + jnp.exp(sink - m)
        weighted.append(((p_buf / denom).astype(BF16), (p_new / denom) * vn))
    for ((z_ref, kb_ref, vb_ref, y_ref, ko_ref, vo_ref), g), (p_buf, o_new) in zip(items, weighted):
        lanes = slice(g * HEAD_DIM, (g + 1) * HEAD_DIM)
        o = jnp.dot(p_buf, vb_ref[:, lanes].astype(BF16), preferred_element_type=F32) + o_new
        for j in range(Q_PER_KV):
            h = g * Q_PER_KV + j
            y_ref[:, 3 * gw + h * HEAD_DIM:3 * gw + (h + 1) * HEAD_DIM] = o[j:j + 1, :].astype(y_ref.dtype)
    for _, kb_ref, vb_ref, _, ko_ref, vo_ref in views:
        wb = kb_ref.shape[0]
        ko_ref[0:wb - 1, :] = kb_ref[1:wb, :]
        vo_ref[0:wb - 1, :] = vb_ref[1:wb, :]


def _mix_sample_abc(z_ref, sa_ref, sp_ref, sc_ref, wa_ref, wp_ref, ps_ref, wc_ref, cb_ref, lg_ref, lb_ref,
                    y_ref, sao_ref, spo_ref, sco_ref, *, start_pos):
    gw = GROUP_W
    zc = lambda c, w=gw: z_ref[:, c:c + w]

    ua = zc(2 * gw) * zc(0)
    conv = wa_ref[0:1, :] * sa_ref[0:1, :] + wa_ref[1:2, :] * sa_ref[1:2, :] + wa_ref[2:3, :] * ua
    y_ref[:, 0:gw] = (zc(gw) * conv).astype(y_ref.dtype)
    sao_ref[0:1, :] = sa_ref[1:2, :]
    sao_ref[1:2, :] = ua

    ub = zc(3 * gw)
    for gi, win in enumerate(POOL_WINDOWS):
        lanes = slice(gi * POOL_GROUP_W, (gi + 1) * POOL_GROUP_W)
        u = ub[:, lanes]
        s = u + jnp.sum(sp_ref[POOL_STATE - (win - 1):POOL_STATE, lanes], axis=0, keepdims=True)
        cnt = float(min(start_pos + 1, win))
        d = s / cnt - u
        yb = jnp.dot(d.astype(BF16), wp_ref[gi], preferred_element_type=F32) * ps_ref[:, lanes]
        y_ref[:, gw + gi * POOL_GROUP_W:gw + (gi + 1) * POOL_GROUP_W] = yb.astype(y_ref.dtype)
    spo_ref[0:POOL_STATE - 1, :] = sp_ref[1:POOL_STATE, :]
    spo_ref[POOL_STATE - 1:POOL_STATE, :] = ub

    uc = zc(4 * gw) * jax.nn.sigmoid(zc(5 * gw))
    kc = CONV_C_W - 1
    acc = jnp.sum(wc_ref[0:kc, :] * sc_ref[...], axis=0, keepdims=True) + wc_ref[kc:kc + 1, :] * uc
    yc = _layer_norm_silu(acc + cb_ref[...], lg_ref[...], lb_ref[...])
    y_ref[:, 2 * gw:3 * gw] = yc.astype(y_ref.dtype)
    sco_ref[0:kc - 1, :] = sc_ref[1:kc, :]
    sco_ref[kc - 1:kc, :] = uc


def mix_sample(z, layer, sa, sp, sc, kb, vb, start_pos, conv_a_w, pool_w, pool_scale, conv_c_w, conv_c_b, ln_g, ln_b,
               q_norm_g, k_norm_g, sinks, bias_row):
    nseq, in_cols = z.shape
    wb = kb.shape[2]
    spb = math.gcd(nseq, SAMPLE_SEQS)
    assert wb == WINDOW
    per_seq = lambda a: pl.BlockSpec((spb,) + a.shape[1:], lambda s: (s,) + (0,) * (a.ndim - 1))
    of_layer = lambda a: pl.BlockSpec((None, spb) + a.shape[2:], lambda s: (layer, s) + (0,) * (a.ndim - 2))
    full = lambda a: pl.BlockSpec(a.shape, lambda s: (0,) * a.ndim)
    z3 = z.reshape(nseq, 1, in_cols)
    new = [jax.ShapeDtypeStruct(a.shape[1:], F32) for a in (sa, sp, sc, kb, vb)]
    outs = pl.pallas_call(
        functools.partial(_mix_sample_kernel, start_pos=start_pos),
        out_shape=(jax.ShapeDtypeStruct((nseq, 1, 4 * GROUP_W), BF16), *new),
        grid=(nseq // spb,),
        in_specs=[pl.BlockSpec(memory_space=pltpu.SMEM), per_seq(z3)] + [
            of_layer(a) for a in (sa, sp, sc, kb, vb)] + [
            full(a) for a in (conv_a_w, pool_w, pool_scale, conv_c_w, conv_c_b, ln_g, ln_b,
                              q_norm_g, k_norm_g, bias_row)],
        out_specs=tuple(per_seq(a) for a in (jax.ShapeDtypeStruct((nseq, 1, 4 * GROUP_W), BF16), *new)),
        compiler_params=_params("parallel"),
        name="mix_sample",
    )(sinks, z3, sa, sp, sc, kb, vb, conv_a_w, pool_w, pool_scale, conv_c_w, conv_c_b, ln_g, ln_b,
      q_norm_g, k_norm_g, bias_row)
    return (outs[0].reshape(nseq, 4 * GROUP_W),) + tuple(outs[1:])


def kernel(x_prompt, x_sample, state_conv_a, state_pool, state_conv_c, cache_k_win, cache_v_win, rel_bias,
           norm_mix_g, w_in, conv_a_w, pool_w, pool_scale, conv_c_w, conv_c_b, ln_c_g, ln_c_b, q_norm_g, k_norm_g,
           attn_sinks, w_out, norm_ffn_g, w_gate, w_up, w_down):
    bsz, seq, d_model = x_prompt.shape
    nseq = x_sample.shape[0]
    depth = w_in.shape[0]
    wb = cache_k_win.shape[2]

    pool_w_b = pool_w.astype(BF16)
    bias = window_bias(rel_bias)
    bias_row = bias[0, np.argsort(_bias_slot_heads()), 0, :]
    kbuf = cache_k_win.reshape(depth, nseq, wb, KV_W)
    vbuf = cache_v_win.reshape(depth, nseq, wb, KV_W)
    row = lambda a, l: a[l][None, :]
    mixer_w = [(conv_a_w[l], pool_w_b[l], row(pool_scale, l), conv_c_w[l], row(conv_c_b, l),
                row(ln_c_g, l), row(ln_c_b, l)) for l in range(depth)]

    xs = x_sample.reshape(nseq, d_model)
    xp = x_prompt.reshape(bsz * seq, d_model)
    new_s = [[] for _ in range(5)]
    new_p = [[] for _ in range(5)]
    hb, ss = norm_prep(xp, row(norm_mix_g, 0), **TILES["norm_prep"])
    w_in_b = w_out_b = w_gate_b = w_up_b = w_down_b = None
    for l in range(depth):
        cast = l if l == 0 else None
        stacked = lambda w, wb_: w if l == 0 else wb_
        tiles = lambda stage: TILES["sample_" + stage + ("" if l == 0 else "_bf16")]
        zs, *made = norm_matmul_small(xs, row(norm_mix_g, l), stacked(w_in, w_in_b), cast, **tiles("in_proj"))
        w_in_b = made[0] if made else w_in_b
        y_s, sa_s, sp_s, sc_s, k_s, v_s = mix_sample(
            zs, l, state_conv_a, state_pool, state_conv_c, kbuf, vbuf, PAST_LEN,
            *mixer_w[l], row(q_norm_g, l), row(k_norm_g, l), attn_sinks[l], bias_row)
        xs, *made = resid_matmul_small(y_s, stacked(w_out, w_out_b), cast, xs, **tiles("out_proj"))
        w_out_b = made[0] if made else w_out_b
        ffs, *made = norm_gateup_small(xs, row(norm_ffn_g, l), stacked(w_gate, w_gate_b), stacked(w_up, w_up_b),
                                       cast, **tiles("gate_up"))
        w_gate_b, w_up_b = made if made else (w_gate_b, w_up_b)
        xs, *made = resid_matmul_small(ffs, stacked(w_down, w_down_b), cast, xs, **tiles("down_proj"))
        w_down_b = made[0] if made else w_down_b
        for i, a in enumerate((sa_s, sp_s, sc_s, k_s, v_s)):
            new_s[i].append(a)

        more = l + 1 < depth
        side = lambda *ws: tuple((w, l + 1, swap) for w, swap in ws) if more else ()
        zp, *nxt_in = scaled_matmul(hb, ss, w_in_b, side((w_in, False)), **TILES["in_proj"])
        y_abc, sa_p, sp_p, sc_p = mix_abc_prompt(zp, bsz, seq, *mixer_w[l], **TILES["mix_abc"])
        y_d, k_p, v_p = attn_prompt(zp, bsz, seq, row(q_norm_g, l), row(k_norm_g, l), attn_sinks[l], bias)
        xp, hb, ss, *nxt_out = resid_matmul((y_abc, y_d), w_out_b, xp, row(norm_ffn_g, l), side((w_out, False)),
                                            **TILES["out_proj"])
        ff, *nxt_ffn = scaled_gateup(hb, ss, w_gate_b, w_up_b, side((w_gate, False), (w_up, False), (w_down, True)),
                                     **TILES["gate_up"])
        if more:
            xp, hb, ss = resid_matmul((ff,), w_down_b, xp, row(norm_mix_g, l + 1), **TILES["down_proj"])
            (w_in_b,), (w_out_b,), (w_gate_b, w_up_b, w_down_b) = nxt_in, nxt_out, nxt_ffn
        else:
            xp, = resid_matmul((ff,), w_down_b, xp, **TILES["down_proj"])
        for i, a in enumerate((sa_p, sp_p, sc_p, k_p, v_p)):
            new_p[i].append(a)

    heads = lambda a: a.reshape(a.shape[:-1] + (N_KV, HEAD_DIM))
    stacked_p = [jnp.stack(a) for a in new_p]
    stacked_s = [jnp.stack(a) for a in new_s]
    return (xp.reshape(bsz, seq, d_model), xs.reshape(nseq, 1, d_model),
            *stacked_p[:3], heads(stacked_p[3]), heads(stacked_p[4]),
            *stacked_s[:3], heads(stacked_s[3]), heads(stacked_s[4]))
```

```python
import functools
import math

import numpy as np
import jax
import jax.numpy as jnp
from jax import lax
from jax.experimental import pallas as pl
from jax.experimental.pallas import tpu as pltpu

F32 = jnp.float32
BF16 = jnp.bfloat16

GROUP_W = 1024
HEAD_DIM = 64
N_HEADS = GROUP_W // HEAD_DIM
N_KV = 4
Q_PER_KV = N_HEADS // N_KV
KV_W = N_KV * HEAD_DIM
WINDOW = 128
CONV_A_W = 3
CONV_C_W = 31
POOL_WINDOWS = (2, 4, 8, 16)
POOL_GROUP_W = GROUP_W // len(POOL_WINDOWS)
POOL_STATE = max(POOL_WINDOWS) - 1
NUM_BUCKETS = 32
MAX_DISTANCE = 128
EPS = 1e-6
NEG_INF = float("-inf")

PAST_LEN = 8192

LANES = 128
SUBLANES = 8
SAMPLE_SEQS = 4

TILES = {
    "norm_prep": dict(tm=512),
    "in_proj": dict(tm=1024, tn=768),
    "out_proj": dict(tm=1024, tn=512),
    "gate_up": dict(tm=2048, tn=256),
    "down_proj": dict(tm=512, tn=512),
    "sample_in_proj": dict(tn=512),
    "sample_out_proj": dict(tn=512),
    "sample_gate_up": dict(tn=256),
    "sample_down_proj": dict(tn=256),
    "sample_in_proj_bf16": dict(tn=1536),
    "sample_out_proj_bf16": dict(tn=1024),
    "sample_gate_up_bf16": dict(tn=256),
    "sample_down_proj_bf16": dict(tn=512),
    "mix_abc": dict(tb=256),
}
MIX_CHUNK = 128
MXU_COLS = 256
HALO = 32
NORM_ROWS = 16
VMEM_LIMIT = 56 << 20


def _params(*sem):
    return pltpu.CompilerParams(dimension_semantics=sem, vmem_limit_bytes=VMEM_LIMIT)


def _rmsnorm_to(x_ref, g_ref, h_ref):
    rows = x_ref.shape[0]
    step = min(NORM_ROWS, rows)

    def body(i, carry):
        r = pl.multiple_of(i * step, step)
        x = x_ref[pl.ds(r, step), :]
        ms = jnp.mean(x * x, axis=-1, keepdims=True)
        h_ref[pl.ds(r, step), :] = (x * lax.rsqrt(ms + EPS) * g_ref[...]).astype(h_ref.dtype)
        return carry

    lax.fori_loop(0, rows // step, body, 0)


def _sample_w_spec(w, layer, tn):
    if layer is None:
        return pl.BlockSpec((w.shape[0], tn), lambda j: (0, j))
    return pl.BlockSpec((None, w.shape[1], tn), lambda j: (layer, 0, j))


def _bf16_tile(w_ref, wb_ref):
    if wb_ref is None:
        return w_ref[...]
    wb_ref[...] = w_ref[...].astype(BF16)
    return wb_ref[...]


def _norm_mm_small_kernel(x_ref, g_ref, w_ref, o_ref, *rest):
    wb_ref, h_ref = rest if len(rest) == 2 else (None, rest[0])

    @pl.when(pl.program_id(0) == 0)
    def _():
        _rmsnorm_to(x_ref, g_ref, h_ref)

    o_ref[...] = jnp.dot(h_ref[...], _bf16_tile(w_ref, wb_ref), preferred_element_type=F32)


def norm_matmul_small(x, g, w, layer, *, tn):
    m, d = x.shape
    n = w.shape[-1]
    assert n % tn == 0
    out_shape = [jax.ShapeDtypeStruct((m, n), F32)]
    out_specs = [pl.BlockSpec((m, tn), lambda j: (0, j))]
    if layer is not None:
        out_shape.append(jax.ShapeDtypeStruct((d, n), BF16))
        out_specs.append(pl.BlockSpec((d, tn), lambda j: (0, j)))
    return pl.pallas_call(
        _norm_mm_small_kernel,
        out_shape=tuple(out_shape),
        grid=(n // tn,),
        in_specs=[pl.BlockSpec((m, d), lambda j: (0, 0)), pl.BlockSpec((1, d), lambda j: (0, 0)),
                  _sample_w_spec(w, layer, tn)],
        out_specs=tuple(out_specs),
        scratch_shapes=[pltpu.VMEM((m, d), BF16)],
        compiler_params=_params("arbitrary"),
        name="sample_in_proj",
    )(x, g, w)


def _norm_gateup_small_kernel(x_ref, g_ref, wg_ref, wu_ref, o_ref, *rest):
    wgb_ref, wub_ref, h_ref = rest if len(rest) == 3 else (None, None, rest[0])

    @pl.when(pl.program_id(0) == 0)
    def _():
        _rmsnorm_to(x_ref, g_ref, h_ref)

    h = h_ref[...]
    a = jnp.dot(h, _bf16_tile(wg_ref, wgb_ref), preferred_element_type=F32)
    b = jnp.dot(h, _bf16_tile(wu_ref, wub_ref), preferred_element_type=F32)
    o_ref[...] = (a * jax.nn.sigmoid(a) * b).astype(o_ref.dtype)


def norm_gateup_small(x, g, w_gate, w_up, layer, *, tn):
    m, d = x.shape
    f = w_gate.shape[-1]
    assert f % tn == 0
    out_shape = [jax.ShapeDtypeStruct((m, f), BF16)]
    out_specs = [pl.BlockSpec((m, tn), lambda j: (0, j))]
    if layer is not None:
        out_shape += [jax.ShapeDtypeStruct((d, f), BF16)] * 2
        out_specs += [pl.BlockSpec((d, tn), lambda j: (0, j))] * 2
    return pl.pallas_call(
        _norm_gateup_small_kernel,
        out_shape=tuple(out_shape),
        grid=(f // tn,),
        in_specs=[pl.BlockSpec((m, d), lambda j: (0, 0)), pl.BlockSpec((1, d), lambda j: (0, 0)),
                  _sample_w_spec(w_gate, layer, tn), _sample_w_spec(w_up, layer, tn)],
        out_specs=tuple(out_specs),
        scratch_shapes=[pltpu.VMEM((m, d), BF16)],
        compiler_params=_params("arbitrary"),
        name="sample_gate_up",
    )(x, g, w_gate, w_up)


def _resid_mm_small_kernel(a_ref, w_ref, r_ref, o_ref, wb_ref=None):
    o_ref[...] = r_ref[...] + jnp.dot(a_ref[...], _bf16_tile(w_ref, wb_ref), preferred_element_type=F32)


def resid_matmul_small(a, w, layer, resid, *, tn):
    m, n = resid.shape
    k = a.shape[1]
    assert w.shape[-2] == k and n % tn == 0
    out_shape = [jax.ShapeDtypeStruct((m, n), F32)]
    out_specs = [pl.BlockSpec((m, tn), lambda j: (0, j))]
    if layer is not None:
        out_shape.append(jax.ShapeDtypeStruct((k, n), BF16))
        out_specs.append(pl.BlockSpec((k, tn), lambda j: (0, j)))
    return pl.pallas_call(
        _resid_mm_small_kernel,
        out_shape=tuple(out_shape),
        grid=(n // tn,),
        in_specs=[pl.BlockSpec((m, k), lambda j: (0, 0)), _sample_w_spec(w, layer, tn),
                  pl.BlockSpec((m, tn), lambda j: (0, j))],
        out_specs=tuple(out_specs),
        compiler_params=_params("arbitrary"),
        name="sample_proj_residual",
    )(a, w, resid)


def _sum_lane_groups(x):
    acc = x[:, 0:LANES]
    for k in range(1, x.shape[1] // LANES):
        acc = acc + x[:, k * LANES:(k + 1) * LANES]
    return acc


def _rowscale_to(ss_ref, rs_ref, d):
    tot = jnp.sum(_sum_lane_groups(ss_ref[...]), axis=-1, keepdims=True)
    rs_ref[...] = jnp.broadcast_to(lax.rsqrt(tot / d + EPS), rs_ref.shape)


def _norm_prep_kernel(x_ref, g_ref, hb_ref, ss_ref):
    rows = x_ref.shape[0]

    def body(i, carry):
        r = pl.multiple_of(i * NORM_ROWS, NORM_ROWS)
        x = x_ref[pl.ds(r, NORM_ROWS), :]
        hb_ref[pl.ds(r, NORM_ROWS), :] = (x * g_ref[...]).astype(BF16)
        ss_ref[pl.ds(r, NORM_ROWS), :] = _sum_lane_groups(x * x)
        return carry

    lax.fori_loop(0, rows // NORM_ROWS, body, 0)


def norm_prep(x, g, *, tm):
    m, d = x.shape
    assert m % tm == 0 and tm % NORM_ROWS == 0
    return pl.pallas_call(
        _norm_prep_kernel,
        out_shape=(jax.ShapeDtypeStruct((m, d), BF16), jax.ShapeDtypeStruct((m, LANES), F32)),
        grid=(m // tm,),
        in_specs=[pl.BlockSpec((tm, d), lambda i: (i, 0)), pl.BlockSpec((1, d), lambda i: (0, 0))],
        out_specs=(pl.BlockSpec((tm, d), lambda i: (i, 0)), pl.BlockSpec((tm, LANES), lambda i: (i, 0))),
        compiler_params=_params("parallel"),
        name="norm_prep",
    )(x, g)


def _side_cast_specs(side_casts, gm, gn):
    in_specs, out_specs, out_shape, args = [], [], [], []
    for w, layer, swap in side_casts:
        _, r, c = w.shape
        gr, gc = (gn, gm) if swap else (gm, gn)
        assert r % (gr * 2 * SUBLANES) == 0 and c % (gc * LANES) == 0
        block = (r // gr, c // gc)
        pick = (lambda i, j: (j, i)) if swap else (lambda i, j: (i, j))
        in_specs.append(pl.BlockSpec((None,) + block, lambda i, j, pick=pick, layer=layer: (layer,) + pick(i, j)))
        out_specs.append(pl.BlockSpec(block, pick))
        out_shape.append(jax.ShapeDtypeStruct((r, c), BF16))
        args.append(w)
    return in_specs, out_specs, out_shape, args


def _run_side_casts(src_refs, dst_refs):
    for src, dst in zip(src_refs, dst_refs):
        dst[...] = src[...].astype(BF16)


def _scaled_mm_kernel(hb_ref, ss_ref, w_ref, *refs):
    n_side = (len(refs) - 2) // 2
    o_ref, rs_ref = refs[n_side], refs[-1]

    @pl.when(pl.program_id(1) == 0)
    def _():
        _rowscale_to(ss_ref, rs_ref, hb_ref.shape[1])

    _run_side_casts(refs[:n_side], refs[n_side + 1:-1])
    acc = jnp.dot(hb_ref[...], w_ref[...], preferred_element_type=F32)
    rs = rs_ref[...]
    for k in range(o_ref.shape[1] // LANES):
        o_ref[:, k * LANES:(k + 1) * LANES] = acc[:, k * LANES:(k + 1) * LANES] * rs


def scaled_matmul(hb, ss, w, side_casts=(), *, tm, tn):
    m, d = hb.shape
    n = w.shape[1]
    assert m % tm == 0 and n % tn == 0 and tn % LANES == 0
    s_in, s_out, s_shape, s_args = _side_cast_specs(side_casts, m // tm, n // tn)
    return pl.pallas_call(
        _scaled_mm_kernel,
        out_shape=(jax.ShapeDtypeStruct((m, n), F32), *s_shape),
        grid=(m // tm, n // tn),
        in_specs=[
            pl.BlockSpec((tm, d), lambda i, j: (i, 0)),
            pl.BlockSpec((tm, ss.shape[1]), lambda i, j: (i, 0)),
            pl.BlockSpec((d, tn), lambda i, j: (0, j)),
            *s_in,
        ],
        out_specs=(pl.BlockSpec((tm, tn), lambda i, j: (i, j)), *s_out),
        scratch_shapes=[pltpu.VMEM((tm, LANES), F32)],
        compiler_params=_params("parallel", "arbitrary"),
        name="in_proj",
    )(hb, ss, w, *s_args)


def _scaled_gateup_kernel(hb_ref, ss_ref, wg_ref, wu_ref, *refs):
    n_side = (len(refs) - 2) // 2
    o_ref, rs_ref = refs[n_side], refs[-1]

    @pl.when(pl.program_id(1) == 0)
    def _():
        _rowscale_to(ss_ref, rs_ref, hb_ref.shape[1])

    _run_side_casts(refs[:n_side], refs[n_side + 1:-1])
    half = hb_ref.shape[0] // 2
    for r0 in (0, half):
        rows = slice(r0, r0 + half)
        h = hb_ref[rows, :]
        a = jnp.dot(h, wg_ref[...], preferred_element_type=F32)
        b = jnp.dot(h, wu_ref[...], preferred_element_type=F32)
        rs = rs_ref[rows, :]
        for k in range(o_ref.shape[1] // LANES):
            cols = slice(k * LANES, (k + 1) * LANES)
            ak = a[:, cols] * rs
            o_ref[rows, cols] = (ak * jax.nn.sigmoid(ak) * (b[:, cols] * rs)).astype(o_ref.dtype)


def scaled_gateup(hb, ss, w_gate, w_up, side_casts=(), *, tm, tn):
    m, d = hb.shape
    f = w_gate.shape[1]
    assert m % tm == 0 and f % tn == 0 and tn % LANES == 0
    w_spec = pl.BlockSpec((d, tn), lambda i, j: (0, j))
    s_in, s_out, s_shape, s_args = _side_cast_specs(side_casts, m // tm, f // tn)
    return pl.pallas_call(
        _scaled_gateup_kernel,
        out_shape=(jax.ShapeDtypeStruct((m, f), BF16), *s_shape),
        grid=(m // tm, f // tn),
        in_specs=[
            pl.BlockSpec((tm, d), lambda i, j: (i, 0)),
            pl.BlockSpec((tm, ss.shape[1]), lambda i, j: (i, 0)),
            w_spec, w_spec, *s_in],
        out_specs=(pl.BlockSpec((tm, tn), lambda i, j: (i, j)), *s_out),
        scratch_shapes=[pltpu.VMEM((tm, LANES), F32)],
        compiler_params=_params("parallel", "arbitrary"),
        name="gate_up",
    )(hb, ss, w_gate, w_up, *s_args)


def _resid_mm_kernel(*refs, k_splits, emit_norm, n_side):
    n_a = len(k_splits)
    a_refs = refs[:n_a]
    n_in = 3 if emit_norm else 2
    n_out = 3 if emit_norm else 1
    main_in = refs[n_a:n_a + n_in]
    side_in = refs[n_a + n_in:n_a + n_in + n_side]
    main_out = refs[n_a + n_in + n_side:n_a + n_in + n_side + n_out]
    side_out = refs[n_a + n_in + n_side + n_out:]
    if emit_norm:
        w_ref, r_ref, g_ref = main_in
        o_ref, hb_ref, ss_ref = main_out
    else:
        w_ref, r_ref = main_in
        o_ref, = main_out
    _run_side_casts(side_in, side_out)
    if emit_norm:
        @pl.when(pl.program_id(1) == 0)
        def _():
            ss_ref[...] = jnp.zeros_like(ss_ref)

    tn = o_ref.shape[1]
    part = None
    for c in range(0, tn, MXU_COLS):
        cols = slice(c, c + MXU_COLS)
        acc = r_ref[:, cols]
        off = 0
        for a_ref, k in zip(a_refs, k_splits):
            acc = acc + jnp.dot(a_ref[...], w_ref[off:off + k, cols], preferred_element_type=F32)
            off += k
        o_ref[:, cols] = acc
        if emit_norm:
            hb_ref[:, cols] = (acc * g_ref[:, cols]).astype(BF16)
            ss = _sum_lane_groups(acc * acc)
            part = ss if part is None else part + ss
    if emit_norm:
        ss_ref[...] += part


def resid_matmul(acts, w, resid, g_next=None, side_casts=(), *, tm, tn):
    m, n = resid.shape
    k_splits = tuple(a.shape[1] for a in acts)
    k_total = sum(k_splits)
    emit_norm = g_next is not None
    assert w.shape[0] == k_total and m % tm == 0 and n % tn == 0 and tn % MXU_COLS == 0
    tile = pl.BlockSpec((tm, tn), lambda i, j: (i, j))
    in_specs = [pl.BlockSpec((tm, k), lambda i, j: (i, 0)) for k in k_splits] + [
        pl.BlockSpec((k_total, tn), lambda i, j: (0, j)), tile]
    out_shape = [jax.ShapeDtypeStruct((m, n), F32)]
    out_specs = [tile]
    args = (*acts, w, resid)
    if emit_norm:
        in_specs.append(pl.BlockSpec((1, tn), lambda i, j: (0, j)))
        out_shape += [jax.ShapeDtypeStruct((m, n), BF16), jax.ShapeDtypeStruct((m, LANES), F32)]
        out_specs += [tile, pl.BlockSpec((tm, LANES), lambda i, j: (i, 0))]
        args = args + (g_next,)
    s_in, s_out, s_shape, s_args = _side_cast_specs(side_casts, m // tm, n // tn)
    return pl.pallas_call(
        functools.partial(_resid_mm_kernel, k_splits=k_splits, emit_norm=emit_norm, n_side=len(s_args)),
        out_shape=(*out_shape, *s_shape),
        grid=(m // tm, n // tn),
        in_specs=in_specs + s_in,
        out_specs=(*out_specs, *s_out),
        compiler_params=_params("parallel", "arbitrary"),
        name="proj_residual",
    )(*args, *s_args)


def _bucket_matrix():
    i = np.arange(WINDOW)[:, None]
    j = np.arange(2 * WINDOW)[None, :]
    rel = i + WINDOW - j
    max_exact = NUM_BUCKETS // 2
    nf = np.maximum(rel, 1).astype(np.float32)
    large = max_exact + (np.log(nf / max_exact) / math.log(MAX_DISTANCE / max_exact)
                         * (NUM_BUCKETS - max_exact)).astype(np.int32)
    large = np.minimum(large, NUM_BUCKETS - 1)
    bucket = np.where(rel < max_exact, rel, large)
    valid = (rel >= 0) & (rel < WINDOW)
    later = np.where(valid, bucket, -1).astype(np.int32)
    first = np.where(j >= WINDOW, later, -1)
    return np.stack([later, first])


def _bias_kernel(rb_ref, bk_ref, o_ref):
    t = pl.program_id(1)
    per = N_HEADS // N_KV
    h = per * (t // per) + 2 * (t % 2) + (t % per) // 2
    bk = bk_ref[...]
    acc = jnp.full(bk.shape, NEG_INF, F32)
    for b in range(NUM_BUCKETS):
        acc = jnp.where(bk == b, rb_ref[b, h], acc)
    o_ref[...] = acc


def window_bias(rel_bias):
    bucket = jnp.asarray(_bucket_matrix())
    return pl.pallas_call(
        _bias_kernel,
        out_shape=jax.ShapeDtypeStruct((2, N_HEADS, WINDOW, 2 * WINDOW), F32),
        grid=(2, N_HEADS),
        in_specs=[
            pl.BlockSpec(memory_space=pltpu.SMEM),
            pl.BlockSpec((None, WINDOW, 2 * WINDOW), lambda f, h: (f, 0, 0)),
        ],
        out_specs=pl.BlockSpec((None, None, WINDOW, 2 * WINDOW), lambda f, h: (f, h, 0, 0)),
        compiler_params=_params("arbitrary", "arbitrary"),
        name="window_bias",
    )(rel_bias, bucket)


def _layer_norm_silu(y, g, b):
    mu = jnp.mean(y, axis=-1, keepdims=True)
    var = jnp.mean(jnp.square(y - mu), axis=-1, keepdims=True)
    y = (y - mu) * lax.rsqrt(var + EPS) * g + b
    return y * jax.nn.sigmoid(y)


def _delay_rows(blocks, b):
    if b == 0:
        return blocks[1:]
    return [jnp.concatenate([blocks[i], blocks[i + 1]], axis=0)[SUBLANES - b:2 * SUBLANES - b]
            for i in range(len(blocks) - 1)]


def _trailing_sum(blocks, n):
    step = 1
    while step < n:
        blocks = [x + y for x, y in zip(blocks[1:], _delay_rows(blocks, step))]
        step *= 2
    return blocks


def _store_bf16_rows(ref, r0, cols, blocks):
    for p in range(len(blocks) // 2):
        pair = jnp.concatenate([blocks[2 * p], blocks[2 * p + 1]], axis=0)
        ref[pl.ds(r0 + 2 * SUBLANES * p, 2 * SUBLANES), cols] = pair.astype(ref.dtype)


def _mix_abc_kernel(ha_ref, ba_ref, ca_ref, ub_ref, ac_ref, gc_ref,
                    wa_ref, wp_ref, ps_ref, wc_ref, cb_ref, lg_ref, lb_ref,
                    y_ref, sa_ref, sp_ref, sc_ref,
                    ea_ref, eb_ref, ec_ref, d_ref, pre_ref, *, tb):
    t = pl.program_id(1)
    nt = pl.num_programs(1)
    nblk = MIX_CHUNK // SUBLANES
    hist = HALO // SUBLANES

    @pl.when(t == 0)
    def _():
        zeros = jnp.zeros((HALO, GROUP_W), F32)
        ea_ref[0:HALO, :] = zeros
        eb_ref[0:HALO, :] = zeros
        ec_ref[0:HALO, :] = zeros

    ea_ref[HALO:HALO + tb, :] = ca_ref[...] * ha_ref[...]
    eb_ref[HALO:HALO + tb, :] = ub_ref[...]
    ec_ref[HALO:HALO + tb, :] = ac_ref[...] * jax.nn.sigmoid(gc_ref[...])

    def chunk(ci, carry):
        r0 = pl.multiple_of(ci * MIX_CHUNK, MIX_CHUNK)
        load = lambda ref, j, cols: ref[pl.ds(r0 + SUBLANES * j, SUBLANES), cols]
        row_pos = t * tb + r0 + lax.broadcasted_iota(jnp.int32, (SUBLANES, LANES), 0)
        for c in range(GROUP_W // LANES):
            cols = slice(c * LANES, (c + 1) * LANES)

            xa = [load(ea_ref, j, cols) for j in range(hist - 1, hist + nblk)]
            conv = [wa_ref[CONV_A_W - 1:CONV_A_W, cols] * x for x in xa[1:]]
            for b in range(1, CONV_A_W):
                w = wa_ref[CONV_A_W - 1 - b:CONV_A_W - b, cols]
                conv = [x + y for x, y in zip(conv, _delay_rows([w * x for x in xa], b))]
            ya = [ba_ref[pl.ds(r0 + SUBLANES * i, SUBLANES), cols] * conv[i] for i in range(nblk)]
            _store_bf16_rows(y_ref, r0, cols, ya)

            win = POOL_WINDOWS[c * LANES // POOL_GROUP_W]
            xb = [load(eb_ref, j, cols) for j in range(hist + nblk)]
            if win > SUBLANES:
                assert win == 2 * SUBLANES
                sums = _trailing_sum([x + y for x, y in zip(xb[1:], xb[:-1])], SUBLANES)
            else:
                sums = _trailing_sum(xb, win)
            sums = sums[len(sums) - nblk:]
            db = []
            for i in range(nblk):
                cnt = jnp.minimum(row_pos + (SUBLANES * i + 1), win).astype(F32)
                db.append(sums[i] / cnt - xb[hist + i])
            _store_bf16_rows(d_ref, r0, cols, db)

            xc = [load(ec_ref, j, cols) for j in range(hist + nblk)]
            acc = None
            for b in range(SUBLANES):
                q = None
                for a in range(hist):
                    delay = SUBLANES * a + b
                    if delay > CONV_C_W - 1:
                        continue
                    w = wc_ref[CONV_C_W - 1 - delay:CONV_C_W - delay, cols]
                    terms = [w * xc[hist - 1 - a + i] for i in range(nblk + 1)]
                    q = terms if q is None else [x + y for x, y in zip(q, terms)]
                q = _delay_rows(q, b)
                acc = q if acc is None else [x + y for x, y in zip(acc, q)]
            for i in range(nblk):
                pre_ref[pl.ds(r0 + SUBLANES * i, SUBLANES), cols] = acc[i] + cb_ref[:, cols]
        return carry

    lax.fori_loop(0, tb // MIX_CHUNK, chunk, 0)

    for gi in range(len(POOL_WINDOWS)):
        lanes = slice(gi * POOL_GROUP_W, (gi + 1) * POOL_GROUP_W)
        yb = jnp.dot(d_ref[:, lanes], wp_ref[gi], preferred_element_type=F32) * ps_ref[:, lanes]
        y_ref[:, GROUP_W + gi * POOL_GROUP_W:GROUP_W + (gi + 1) * POOL_GROUP_W] = yb.astype(y_ref.dtype)

    yc = _layer_norm_silu(pre_ref[...], lg_ref[...], lb_ref[...])
    y_ref[:, 2 * GROUP_W:3 * GROUP_W] = yc.astype(y_ref.dtype)

    @pl.when(t == nt - 1)
    def _():
        end = HALO + tb
        sa_ref[...] = ea_ref[end - (CONV_A_W - 1):end, :]
        sp_ref[...] = eb_ref[end - POOL_STATE:end, :]
        sc_ref[...] = ec_ref[end - (CONV_C_W - 1):end, :]

    ea_ref[0:HALO, :] = ea_ref[tb:tb + HALO, :]
    eb_ref[0:HALO, :] = eb_ref[tb:tb + HALO, :]
    ec_ref[0:HALO, :] = ec_ref[tb:tb + HALO, :]


def mix_abc_prompt(z, bsz, seq, conv_a_w, pool_w, pool_scale, conv_c_w, conv_c_b, ln_g, ln_b, *, tb):
    assert seq % tb == 0 and tb >= HALO and tb % MIX_CHUNK == 0
    assert HALO % SUBLANES == 0 and HALO >= CONV_C_W - 1 and HALO >= 2 * SUBLANES >= max(POOL_WINDOWS)
    nt = seq // tb
    col = lambda c: pl.BlockSpec((tb, GROUP_W), lambda b, t: (b * nt + t, c))
    full = lambda a: pl.BlockSpec(a.shape, lambda b, t: (0,) * a.ndim)
    state = lambda rows: pl.BlockSpec((None, rows, GROUP_W), lambda b, t: (b, 0, 0))
    ext = pltpu.VMEM((HALO + tb, GROUP_W), F32)
    return pl.pallas_call(
        functools.partial(_mix_abc_kernel, tb=tb),
        out_shape=(
            jax.ShapeDtypeStruct((bsz * seq, 3 * GROUP_W), BF16),
            jax.ShapeDtypeStruct((bsz, CONV_A_W - 1, GROUP_W), F32),
            jax.ShapeDtypeStruct((bsz, POOL_STATE, GROUP_W), F32),
            jax.ShapeDtypeStruct((bsz, CONV_C_W - 1, GROUP_W), F32),
        ),
        grid=(bsz, nt),
        in_specs=[col(c) for c in range(6)] + [
            full(conv_a_w), full(pool_w), full(pool_scale), full(conv_c_w),
            full(conv_c_b), full(ln_g), full(ln_b)],
        out_specs=(
            pl.BlockSpec((tb, 3 * GROUP_W), lambda b, t: (b * nt + t, 0)),
            state(CONV_A_W - 1), state(POOL_STATE), state(CONV_C_W - 1),
        ),
        scratch_shapes=[ext, ext, ext, pltpu.VMEM((tb, GROUP_W), BF16), pltpu.VMEM((tb, GROUP_W), F32)],
        compiler_params=_params("parallel", "arbitrary"),
        name="mix_abc_prompt",
    )(z, z, z, z, z, z, conv_a_w, pool_w, pool_scale, conv_c_w, conv_c_b, ln_g, ln_b)


def _head_rmsnorm(x, g, scale=1.0):
    ms = jnp.mean(x * x, axis=-1, keepdims=True)
    return x * (lax.rsqrt(ms + EPS) * scale) * g


def _headwise_rmsnorm(x, e, g):
    x2 = x * x
    hi = x2.astype(BF16)
    lo = (x2 - hi.astype(F32)).astype(BF16)
    sums = []
    for c in range(0, x.shape[1], MXU_COLS):
        cols = slice(c, c + MXU_COLS)
        sums.append(jnp.dot(hi[:, cols], e, preferred_element_type=F32)
                    + jnp.dot(lo[:, cols], e, preferred_element_type=F32))
    ss = jnp.concatenate(sums, axis=1)
    return x * lax.rsqrt(ss * (1.0 / HEAD_DIM) + EPS) * g


def _attn_prompt_kernel(sink_ref, q_ref, kv_ref, qg_ref, kg_ref, e_ref, bias_ref,
                        y_ref, ko_ref, vo_ref, kp_ref, vp_ref, s_ref, rhs_ref, p_ref, es_ref):
    n = pl.program_id(1)

    @pl.when(n == 0)
    def _():
        kp_ref[...] = jnp.zeros_like(kp_ref)
        vp_ref[...] = jnp.zeros_like(vp_ref)

    first = (n == 0).astype(jnp.int32)
    e = e_ref[...]
    qn = _headwise_rmsnorm(q_ref[...], e, qg_ref[...]).astype(BF16)
    kn = _headwise_rmsnorm(kv_ref[:, 0:KV_W], e, kg_ref[...])
    v = kv_ref[:, KV_W:2 * KV_W]
    ko_ref[...] = kn
    vo_ref[...] = v
    knb = kn.astype(BF16)
    vb = v.astype(BF16)

    low_q = lax.broadcasted_iota(jnp.int32, (2 * WINDOW, LANES), 1) < HEAD_DIM
    ones_half = (jnp.where(low_q, 1.0, 0.0).astype(BF16), jnp.where(low_q, 0.0, 1.0).astype(BF16))
    keep_lo, keep_hi = ones_half
    tiles = [(g, half) for g in range(N_KV) for half in range(2)]

    for i, (g, half) in enumerate(tiles):
        if half == 0:
            lanes = slice((g // 2) * LANES, (g // 2 + 1) * LANES)
            k_slab = jnp.concatenate([kp_ref[:, lanes], knb[:, lanes]], axis=0)
            v_slab = jnp.concatenate([vp_ref[:, lanes], vb[:, lanes]], axis=0)
            k_swap = pltpu.roll(k_slab, HEAD_DIM, axis=1)
            v_swap = pltpu.roll(v_slab, HEAD_DIM, axis=1)
            if g % 2 == 0:
                k_half = (k_slab * keep_lo, k_swap * keep_hi)
                v_half = (v_slab * keep_lo, v_swap * keep_hi)
            else:
                k_half = (k_swap * keep_lo, k_slab * keep_hi)
                v_half = (v_swap * keep_lo, v_slab * keep_hi)
            s0 = 2 * g
            q2 = jnp.concatenate([qn[:, s0 * LANES:(s0 + 1) * LANES], qn[:, (s0 + 1) * LANES:(s0 + 2) * LANES]],
                                 axis=0)
        s_ref[i] = lax.dot_general(q2, k_half[half], (((1,), (1,)), ((), ())), preferred_element_type=F32)
        rhs_ref[i] = jnp.concatenate([v_half[half], ones_half[half]], axis=1)

    for i, (g, half) in enumerate(tiles):
        heads = (4 * g + half, 4 * g + 2 + half)
        t = N_HEADS // N_KV * g + 2 * half
        s = s_ref[i] + bias_ref[first, t:t + 2].reshape(2 * WINDOW, 2 * WINDOW)
        sink = jnp.concatenate([jnp.full((WINDOW, LANES), sink_ref[h], F32) for h in heads], axis=0)
        m = jnp.maximum(jnp.broadcast_to(jnp.max(s, axis=-1, keepdims=True), (2 * WINDOW, LANES)), sink)
        p_ref[i] = jnp.exp(s - jnp.concatenate([m, m], axis=1)).astype(BF16)
        es_ref[i] = jnp.exp(sink - m)

    for g in range(N_KV):
        lo, hi = 2 * g, 2 * g + 1
        acc = (jnp.dot(p_ref[lo], rhs_ref[lo], preferred_element_type=F32)
               + jnp.dot(p_ref[hi], rhs_ref[hi], preferred_element_type=F32))
        denom = acc[:, LANES:] + jnp.where(low_q, es_ref[lo], es_ref[hi])
        out = (acc[:, 0:LANES] / denom).astype(y_ref.dtype)
        y_ref[:, lo * LANES:(lo + 1) * LANES] = out[0:WINDOW]
        y_ref[:, hi * LANES:(hi + 1) * LANES] = out[WINDOW:2 * WINDOW]

    kp_ref[...] = knb
    vp_ref[...] = vb


def _bias_slot_heads():
    per = N_HEADS // N_KV
    return [per * (t // per) + 2 * (t % 2) + (t % per) // 2 for t in range(N_HEADS)]


def attn_prompt(z, bsz, seq, q_norm_g, k_norm_g, sinks, bias):
    assert seq % WINDOW == 0
    nb = seq // WINDOW
    q_col = 6 * GROUP_W // GROUP_W
    kv_col = 7 * GROUP_W // (2 * KV_W)
    qg = jnp.tile(q_norm_g, (1, N_HEADS)) * HEAD_DIM ** -0.5
    kg = jnp.tile(k_norm_g, (1, N_KV))
    e = jnp.asarray(np.kron(np.eye(MXU_COLS // HEAD_DIM), np.ones((HEAD_DIM, HEAD_DIM))), BF16)
    win = pl.BlockSpec((None, WINDOW, KV_W), lambda b, n: (b, 0, 0))
    full = lambda a: pl.BlockSpec(a.shape, lambda b, n: (0,) * a.ndim)
    return pl.pallas_call(
        _attn_prompt_kernel,
        out_shape=(
            jax.ShapeDtypeStruct((bsz * seq, GROUP_W), BF16),
            jax.ShapeDtypeStruct((bsz, WINDOW, KV_W), F32),
            jax.ShapeDtypeStruct((bsz, WINDOW, KV_W), F32),
        ),
        grid=(bsz, nb),
        in_specs=[
            pl.BlockSpec(memory_space=pltpu.SMEM),
            pl.BlockSpec((WINDOW, GROUP_W), lambda b, n: (b * nb + n, q_col)),
            pl.BlockSpec((WINDOW, 2 * KV_W), lambda b, n: (b * nb + n, kv_col)),
            full(qg), full(kg), full(e), full(bias),
        ],
        out_specs=(pl.BlockSpec((WINDOW, GROUP_W), lambda b, n: (b * nb + n, 0)), win, win),
        scratch_shapes=[
            pltpu.VMEM((WINDOW, KV_W), BF16), pltpu.VMEM((WINDOW, KV_W), BF16),
            pltpu.VMEM((2 * N_KV, 2 * WINDOW, 2 * WINDOW), F32),
            pltpu.VMEM((2 * N_KV, 2 * WINDOW, 2 * LANES), BF16),
            pltpu.VMEM((2 * N_KV, 2 * WINDOW, 2 * WINDOW), BF16),
            pltpu.VMEM((2 * N_KV, 2 * WINDOW, LANES), F32),
        ],
        compiler_params=_params("parallel", "arbitrary"),
        name="attn_prompt",
    )(sinks, z, z, qg, kg, e, bias)


def _mix_sample_kernel(sink_ref, *refs, start_pos):
    seq_in, shared, seq_out = refs[:6], refs[6:16], refs[16:]
    nseq = seq_in[0].shape[0]
    views = [tuple(r.at[s] for r in seq_in) + tuple(r.at[s] for r in seq_out) for s in range(nseq)]
    for z_ref, sa_ref, sp_ref, sc_ref, _, _, y_ref, sao_ref, spo_ref, sco_ref, _, _ in views:
        _mix_sample_abc(z_ref, sa_ref, sp_ref, sc_ref, *shared[:7], y_ref, sao_ref, spo_ref, sco_ref,
                        start_pos=start_pos)
    _mix_sample_attn(sink_ref, [(v[0], v[4], v[5], v[6], v[10], v[11]) for v in views], *shared[7:])


def _mix_sample_attn(sink_ref, views, qg_ref, kg_ref, bias_ref):
    gw = GROUP_W
    items = [(v, g) for v in views for g in range(N_KV)]
    scored = []
    for (z_ref, kb_ref, vb_ref, y_ref, ko_ref, vo_ref), g in items:
        wb = kb_ref.shape[0]
        zc = lambda c, w: z_ref[:, c:c + w]
        lanes = slice(g * HEAD_DIM, (g + 1) * HEAD_DIM)
        kn = _head_rmsnorm(zc(7 * gw + g * HEAD_DIM, HEAD_DIM), kg_ref[...])
        vn = zc(7 * gw + KV_W + g * HEAD_DIM, HEAD_DIM)
        ko_ref[wb - 1:wb, lanes] = kn
        vo_ref[wb - 1:wb, lanes] = vn
        qs, sinks = [], []
        for j in range(Q_PER_KV):
            h = g * Q_PER_KV + j
            qs.append(_head_rmsnorm(zc(6 * gw + h * HEAD_DIM, HEAD_DIM), qg_ref[...], HEAD_DIM ** -0.5))
            sinks.append(jnp.full((1, 1), sink_ref[h], F32))
        qcat = jnp.concatenate(qs, axis=0)
        sink = jnp.concatenate(sinks, axis=0)
        hb = bias_ref[g * Q_PER_KV:(g + 1) * Q_PER_KV, :]
        s_buf = lax.dot_general(qcat.astype(BF16), kb_ref[:, lanes].astype(BF16), (((1,), (1,)), ((), ())),
                                preferred_element_type=F32) + hb[:, WINDOW - wb:WINDOW]
        s_new = jnp.sum(qcat * kn, axis=-1, keepdims=True) + hb[:, WINDOW:WINDOW + 1]
        scored.append((s_buf, s_new, sink, vn))
    weighted = []
    for s_buf, s_new, sink, vn in scored:
        m = jnp.maximum(jnp.maximum(jnp.max(s_buf, axis=-1, keepdims=True), s_new), sink)
        p_buf = jnp.exp(s_buf - m)
        p_new = jnp.exp(s_new - m)
        denom = jnp.sum(p_buf, axis=-1, keepdims=True) + p_new + jnp.exp(sink - m)
        weighted.append(((p_buf / denom).astype(BF16), (p_new / denom) * vn))
    for ((z_ref, kb_ref, vb_ref, y_ref, ko_ref, vo_ref), g), (p_buf, o_new) in zip(items, weighted):
        lanes = slice(g * HEAD_DIM, (g + 1) * HEAD_DIM)
        o = jnp.dot(p_buf, vb_ref[:, lanes].astype(BF16), preferred_element_type=F32) + o_new
        for j in range(Q_PER_KV):
            h = g * Q_PER_KV + j
            y_ref[:, 3 * gw + h * HEAD_DIM:3 * gw + (h + 1) * HEAD_DIM] = o[j:j + 1, :].astype(y_ref.dtype)
    for _, kb_ref, vb_ref, _, ko_ref, vo_ref in views:
        wb = kb_ref.shape[0]
        ko_ref[0:wb - 1, :] = kb_ref[1:wb, :]
        vo_ref[0:wb - 1, :] = vb_ref[1:wb, :]


def _mix_sample_abc(z_ref, sa_ref, sp_ref, sc_ref, wa_ref, wp_ref, ps_ref, wc_ref, cb_ref, lg_ref, lb_ref,
                    y_ref, sao_ref, spo_ref, sco_ref, *, start_pos):
    gw = GROUP_W
    zc = lambda c, w=gw: z_ref[:, c:c + w]

    ua = zc(2 * gw) * zc(0)
    conv = wa_ref[0:1, :] * sa_ref[0:1, :] + wa_ref[1:2, :] * sa_ref[1:2, :] + wa_ref[2:3, :] * ua
    y_ref[:, 0:gw] = (zc(gw) * conv).astype(y_ref.dtype)
    sao_ref[0:1, :] = sa_ref[1:2, :]
    sao_ref[1:2, :] = ua

    ub = zc(3 * gw)
    for gi, win in enumerate(POOL_WINDOWS):
        lanes = slice(gi * POOL_GROUP_W, (gi + 1) * POOL_GROUP_W)
        u = ub[:, lanes]
        s = u + jnp.sum(sp_ref[POOL_STATE - (win - 1):POOL_STATE, lanes], axis=0, keepdims=True)
        cnt = float(min(start_pos + 1, win))
        d = s / cnt - u
        yb = jnp.dot(d.astype(BF16), wp_ref[gi], preferred_element_type=F32) * ps_ref[:, lanes]
        y_ref[:, gw + gi * POOL_GROUP_W:gw + (gi + 1) * POOL_GROUP_W] = yb.astype(y_ref.dtype)
    spo_ref[0:POOL_STATE - 1, :] = sp_ref[1:POOL_STATE, :]
    spo_ref[POOL_STATE - 1:POOL_STATE, :] = ub

    uc = zc(4 * gw) * jax.nn.sigmoid(zc(5 * gw))
    kc = CONV_C_W - 1
    acc = jnp.sum(wc_ref[0:kc, :] * sc_ref[...], axis=0, keepdims=True) + wc_ref[kc:kc + 1, :] * uc
    yc = _layer_norm_silu(acc + cb_ref[...], lg_ref[...], lb_ref[...])
    y_ref[:, 2 * gw:3 * gw] = yc.astype(y_ref.dtype)
    sco_ref[0:kc - 1, :] = sc_ref[1:kc, :]
    sco_ref[kc - 1:kc, :] = uc


def mix_sample(z, layer, sa, sp, sc, kb, vb, start_pos, conv_a_w, pool_w, pool_scale, conv_c_w, conv_c_b, ln_g, ln_b,
               q_norm_g, k_norm_g, sinks, bias_row):
    nseq, in_cols = z.shape
    wb = kb.shape[2]
    spb = math.gcd(nseq, SAMPLE_SEQS)
    assert wb == WINDOW
    per_seq = lambda a: pl.BlockSpec((spb,) + a.shape[1:], lambda s: (s,) + (0,) * (a.ndim - 1))
    of_layer = lambda a: pl.BlockSpec((None, spb) + a.shape[2:], lambda s: (layer, s) + (0,) * (a.ndim - 2))
    full = lambda a: pl.BlockSpec(a.shape, lambda s: (0,) * a.ndim)
    z3 = z.reshape(nseq, 1, in_cols)
    new = [jax.ShapeDtypeStruct(a.shape[1:], F32) for a in (sa, sp, sc, kb, vb)]
    outs = pl.pallas_call(
        functools.partial(_mix_sample_kernel, start_pos=start_pos),
        out_shape=(jax.ShapeDtypeStruct((nseq, 1, 4 * GROUP_W), BF16), *new),
        grid=(nseq // spb,),
        in_specs=[pl.BlockSpec(memory_space=pltpu.SMEM), per_seq(z3)] + [
            of_layer(a) for a in (sa, sp, sc, kb, vb)] + [
            full(a) for a in (conv_a_w, pool_w, pool_scale, conv_c_w, conv_c_b, ln_g, ln_b,
                              q_norm_g, k_norm_g, bias_row)],
        out_specs=tuple(per_seq(a) for a in (jax.ShapeDtypeStruct((nseq, 1, 4 * GROUP_W), BF16), *new)),
        compiler_params=_params("parallel"),
        name="mix_sample",
    )(sinks, z3, sa, sp, sc, kb, vb, conv_a_w, pool_w, pool_scale, conv_c_w, conv_c_b, ln_g, ln_b,
      q_norm_g, k_norm_g, bias_row)
    return (outs[0].reshape(nseq, 4 * GROUP_W),) + tuple(outs[1:])


def kernel(x_prompt, x_sample, state_conv_a, state_pool, state_conv_c, cache_k_win, cache_v_win, rel_bias,
           norm_mix_g, w_in, conv_a_w, pool_w, pool_scale, conv_c_w, conv_c_b, ln_c_g, ln_c_b, q_norm_g, k_norm_g,
           attn_sinks, w_out, norm_ffn_g, w_gate, w_up, w_down):
    bsz, seq, d_model = x_prompt.shape
    nseq = x_sample.shape[0]
    depth = w_in.shape[0]
    wb = cache_k_win.shape[2]

    pool_w_b = pool_w.astype(BF16)
    bias = window_bias(rel_bias)
    bias_row = bias[0, np.argsort(_bias_slot_heads()), 0, :]
    kbuf = cache_k_win.reshape(depth, nseq, wb, KV_W)
    vbuf = cache_v_win.reshape(depth, nseq, wb, KV_W)
    row = lambda a, l: a[l][None, :]
    mixer_w = [(conv_a_w[l], pool_w_b[l], row(pool_scale, l), conv_c_w[l], row(conv_c_b, l),
                row(ln_c_g, l), row(ln_c_b, l)) for l in range(depth)]

    xs = x_sample.reshape(nseq, d_model)
    xp = x_prompt.reshape(bsz * seq, d_model)
    new_s = [[] for _ in range(5)]
    new_p = [[] for _ in range(5)]
    hb, ss = norm_prep(xp, row(norm_mix_g, 0), **TILES["norm_prep"])
    w_in_b = w_out_b = w_gate_b = w_up_b = w_down_b = None
    for l in range(depth):
        cast = l if l == 0 else None
        stacked = lambda w, wb_: w if l == 0 else wb_
        tiles = lambda stage: TILES["sample_" + stage + ("" if l == 0 else "_bf16")]
        zs, *made = norm_matmul_small(xs, row(norm_mix_g, l), stacked(w_in, w_in_b), cast, **tiles("in_proj"))
        w_in_b = made[0] if made else w_in_b
        y_s, sa_s, sp_s, sc_s, k_s, v_s = mix_sample(
            zs, l, state_conv_a, state_pool, state_conv_c, kbuf, vbuf, PAST_LEN,
            *mixer_w[l], row(q_norm_g, l), row(k_norm_g, l), attn_sinks[l], bias_row)
        xs, *made = resid_matmul_small(y_s, stacked(w_out, w_out_b), cast, xs, **tiles("out_proj"))
        w_out_b = made[0] if made else w_out_b
        ffs, *made = norm_gateup_small(xs, row(norm_ffn_g, l), stacked(w_gate, w_gate_b), stacked(w_up, w_up_b),
                                       cast, **tiles("gate_up"))
        w_gate_b, w_up_b = made if made else (w_gate_b, w_up_b)
        xs, *made = resid_matmul_small(ffs, stacked(w_down, w_down_b), cast, xs, **tiles("down_proj"))
        w_down_b = made[0] if made else w_down_b
        for i, a in enumerate((sa_s, sp_s, sc_s, k_s, v_s)):
            new_s[i].append(a)

        more = l + 1 < depth
        side = lambda *ws: tuple((w, l + 1, swap) for w, swap in ws) if more else ()
        zp, *nxt_in = scaled_matmul(hb, ss, w_in_b, side((w_in, False)), **TILES["in_proj"])
        y_abc, sa_p, sp_p, sc_p = mix_abc_prompt(zp, bsz, seq, *mixer_w[l], **TILES["mix_abc"])
        y_d, k_p, v_p = attn_prompt(zp, bsz, seq, row(q_norm_g, l), row(k_norm_g, l), attn_sinks[l], bias)
        xp, hb, ss, *nxt_out = resid_matmul((y_abc, y_d), w_out_b, xp, row(norm_ffn_g, l), side((w_out, False)),
                                            **TILES["out_proj"])
        ff, *nxt_ffn = scaled_gateup(hb, ss, w_gate_b, w_up_b, side((w_gate, False), (w_up, False), (w_down, True)),
                                     **TILES["gate_up"])
        if more:
            xp, hb, ss = resid_matmul((ff,), w_down_b, xp, row(norm_mix_g, l + 1), **TILES["down_proj"])
            (w_in_b,), (w_out_b,), (w_gate_b, w_up_b, w_down_b) = nxt_in, nxt_out, nxt_ffn
        else:
            xp, = resid_matmul((ff,), w_down_b, xp, **TILES["down_proj"])
        for i, a in enumerate((sa_p, sp_p, sc_p, k_p, v_p)):
            new_p[i].append(a)

    heads = lambda a: a.reshape(a.shape[:-1] + (N_KV, HEAD_DIM))
    stacked_p = [jnp.stack(a) for a in new_p]
    stacked_s = [jnp.stack(a) for a in new_s]
    return (xp.reshape(bsz, seq, d_model), xs.reshape(nseq, 1, d_model),
            *stacked_p[:3], heads(stacked_p[3]), heads(stacked_p[4]),
            *stacked_s[:3], heads(stacked_s[3]), heads(stacked_s[4]))
```
